```python
import math
import jax, jax.numpy as jnp
from jax import lax
import numpy as np

D_MODEL = 2048
BATCH = 4
SEQ = 2048
DEPTH = 4
DEC_BATCH = 8
DEC_SEQ = 4
PAST_LEN = 16384
PAGE_SIZE = 128

HD = 64
D_ATT = D_MODEL // 2
H_A = D_ATT // HD
DILATED_PATTERNS = ((128, 1), (512, 4), (2048, 16))
MAX_WINDOW = 2048
ROPE_THETA = 10000.0
D_SSD = D_MODEL // 2
SSD_HEAD_DIM = 64
H_S = D_SSD // SSD_HEAD_DIM
SSD_STATE = 128
SSD_GROUPS = 4
SSD_CONV = 4
SSD_CHUNK = 128
D_XBC = D_SSD + 2 * SSD_GROUPS * SSD_STATE
D_MIX = D_ATT + D_SSD
D_IN_PROJ = 3 * D_ATT + D_SSD + D_XBC + H_S
IN_SPLITS = (D_ATT, 2 * D_ATT, 3 * D_ATT, 3 * D_ATT + D_SSD, 3 * D_ATT + D_SSD + D_XBC)
D_FF = 11 * D_MODEL // 4
FFN_CONV = 3
EPS = 1e-6

kernel_name = 'hybrid_dilated_attn_ssd_convffn_step'


def rms_norm(x, g):
    xf = x.astype(jnp.float32)
    y = xf * lax.rsqrt(jnp.mean(xf * xf, axis=-1, keepdims=True) + EPS)
    return (y * g.astype(jnp.float32)).astype(x.dtype)


def rope(x, pos):
    half = HD // 2
    inv = ROPE_THETA ** (-jnp.arange(half, dtype=jnp.float32) / half)
    ang = pos.astype(jnp.float32)[:, None] * inv[None, :]
    cos = jnp.cos(ang)[None, :, None, :]
    sin = jnp.sin(ang)[None, :, None, :]
    x1, x2 = x[..., :half], x[..., half:]
    return jnp.concatenate([x1 * cos - x2 * sin, x2 * cos + x1 * sin], axis=-1)


def causal_dwconv(x, buf, w, b):
    width = w.shape[0]
    L = x.shape[1]
    xp = jnp.concatenate([buf.astype(x.dtype), x], axis=1)
    y = b
    for t in range(width):
        y = y + xp[:, t:t + L] * w[t]
    return y, xp[:, xp.shape[1] - (width - 1):]


def dilated_attn_prompt(q, k, v, window, dil):
    bsz, S, H, hd = q.shape
    M = S // dil
    n = window // dil
    nb = -(-M // n)
    pad = nb * n - M

    def sub(a, front):
        a = a.reshape(bsz, M, dil, H, hd)
        a = jnp.pad(a, ((0, 0), (front, pad), (0, 0), (0, 0), (0, 0)))
        return a.reshape(bsz, -1, n, dil, H, hd)

    qb = sub(q, 0)
    kb = sub(k, n)
    vb = sub(v, n)
    kk = jnp.concatenate([kb[:, :-1], kb[:, 1:]], axis=2)
    vv = jnp.concatenate([vb[:, :-1], vb[:, 1:]], axis=2)
    s = jnp.einsum('bcqrhd,bckrhd->bcrhqk', qb, kk) * (hd ** -0.5)
    qi = jnp.arange(n)[:, None]
    ki = jnp.arange(2 * n)[None, :]
    dist = n + qi - ki
    key_pos = (jnp.arange(nb)[:, None, None] - 1) * n + ki
    valid = (dist >= 0) & (dist <= n) & (key_pos >= 0)
    s = jnp.where(valid[None, :, None, None], s, -jnp.inf)
    m = jnp.max(s, axis=-1, keepdims=True)
    p = jnp.exp(s - m)
    den = jnp.sum(p, axis=-1, keepdims=True)
    o = jnp.einsum('bcrhqk,bckrhd->bcqrhd', p / den, vv)
    lse = (m + jnp.log(den))[..., 0]
    o = o.reshape(bsz, nb * n, dil, H, hd)[:, :M].reshape(bsz, S, H, hd)
    lse = jnp.moveaxis(lse, -1, 2).reshape(bsz, nb * n, dil, H)[:, :M].reshape(bsz, S, H)
    return o, lse


def dilated_attn_step(q, k_all, v_all, window, dil):
    L = q.shape[1]
    hd = q.shape[-1]
    Lb = k_all.shape[1] - L
    n = window // dil
    idx = Lb + jnp.arange(L)[:, None] - dil * jnp.arange(n + 1)[None, :]
    valid = idx >= 0
    idx = jnp.maximum(idx, 0)
    kg = k_all[:, idx]
    vg = v_all[:, idx]
    s = jnp.einsum('blhd,blihd->blhi', q, kg) * (hd ** -0.5)
    s = jnp.where(valid[None, :, None, :], s, -jnp.inf)
    m = jnp.max(s, axis=-1, keepdims=True)
    p = jnp.exp(s - m)
    den = jnp.sum(p, axis=-1, keepdims=True)
    o = jnp.einsum('blhi,blihd->blhd', p / den, vg)
    return o, (m + jnp.log(den))[..., 0]


def attn_mixer(q, k, v, pos, q_norm_g, k_norm_g, win_k, win_v):
    bsz, L, _ = q.shape
    f32 = jnp.float32
    q = rope(rms_norm(q.reshape(bsz, L, H_A, HD).astype(f32), q_norm_g), pos)
    k = rope(rms_norm(k.reshape(bsz, L, H_A, HD).astype(f32), k_norm_g), pos)
    v = v.reshape(bsz, L, H_A, HD).astype(f32)
    if win_k is None:
        res = [dilated_attn_prompt(q, k, v, w, d) for (w, d) in DILATED_PATTERNS]
    else:
        k_all = jnp.concatenate([win_k.astype(f32), k], axis=1)
        v_all = jnp.concatenate([win_v.astype(f32), v], axis=1)
        res = [dilated_attn_step(q, k_all, v_all, w, d) for (w, d) in DILATED_PATTERNS]
    outs = jnp.stack([r[0] for r in res])
    lses = jnp.stack([r[1] for r in res])
    wts = jax.nn.softmax(lses, axis=0)
    out = jnp.sum(outs * wts[..., None], axis=0)
    keep = min(MAX_WINDOW, L)
    return out.reshape(bsz, L, D_ATT), k[:, L - keep:], v[:, L - keep:]


def ssd_chunked(x, a, B, C, h0, chunk):
    b, l, h, p = x.shape
    n = B.shape[-1]
    c = l // chunk
    x = x.reshape(b, c, chunk, h, p)
    a = a.reshape(b, c, chunk, h)
    B = B.reshape(b, c, chunk, h, n)
    C = C.reshape(b, c, chunk, h, n)
    a_cum = jnp.cumsum(a, axis=2)
    seg = a_cum[:, :, :, None, :] - a_cum[:, :, None, :, :]
    causal = jnp.tril(jnp.ones((chunk, chunk), dtype=bool))[None, None, :, :, None]
    decay = jnp.exp(jnp.where(causal, seg, -jnp.inf))
    cb = jnp.einsum('bcihn,bcjhn->bcijh', C, B)
    y_diag = jnp.einsum('bcijh,bcjhp->bcihp', cb * decay, x)
    decay_end = jnp.exp(a_cum[:, :, -1:, :] - a_cum)
    states = jnp.einsum('bcjhn,bcjh,bcjhp->bchpn', B, decay_end, x)
    chunk_decay = jnp.exp(a_cum[:, :, -1, :])

    def step(h_prev, inp):
        dec, st = inp
        return dec[:, :, None, None] * h_prev + st, h_prev

    h_last, h_in = lax.scan(step, h0, (jnp.moveaxis(chunk_decay, 1, 0), jnp.moveaxis(states, 1, 0)))
    h_in = jnp.moveaxis(h_in, 0, 1)
    y_off = jnp.einsum('bcihn,bchpn->bcihp', C, h_in) * jnp.exp(a_cum)[..., None]
    return (y_diag + y_off).reshape(b, l, h, p), h_last


def ssd_mixer(z, xbc, dt_raw, conv_buf, h0, conv_w, conv_b, dt_bias, a_log, d_skip, norm_g):
    f32 = jnp.float32
    bsz, L, _ = z.shape
    xbc, new_buf = causal_dwconv(xbc, conv_buf, conv_w, conv_b)
    xbc = jax.nn.silu(xbc.astype(f32))
    gn = SSD_GROUPS * SSD_STATE
    xs = xbc[..., :D_SSD].reshape(bsz, L, H_S, SSD_HEAD_DIM)
    Bm = xbc[..., D_SSD:D_SSD + gn].reshape(bsz, L, SSD_GROUPS, SSD_STATE)
    Cm = xbc[..., D_SSD + gn:].reshape(bsz, L, SSD_GROUPS, SSD_STATE)
    Bh = jnp.repeat(Bm, H_S // SSD_GROUPS, axis=2)
    Ch = jnp.repeat(Cm, H_S // SSD_GROUPS, axis=2)
    dt = jax.nn.softplus(dt_raw.astype(f32) + dt_bias.astype(f32))
    A = -jnp.exp(a_log.astype(f32))
    chunk = math.gcd(L, SSD_CHUNK)
    y, h_last = ssd_chunked(xs * dt[..., None], dt * A, Bh, Ch, h0.astype(f32), chunk)
    y = y + d_skip.astype(f32)[:, None] * xs
    y = y.reshape(bsz, L, D_SSD) * jax.nn.silu(z.astype(f32))
    return rms_norm(y, norm_g), new_buf, h_last


def trunk_layer(x, pos, win_k, win_v, ssd_buf, ssd_h0, ffn_buf,
                norm1_g, w_in, q_norm_g, k_norm_g, ssd_conv_w, ssd_conv_b,
                ssd_dt_bias, ssd_a_log, ssd_d, ssd_norm_g, w_out,
                norm2_g, w_up, ffn_conv_w, ffn_conv_b, w_down):
    proj = rms_norm(x, norm1_g) @ w_in
    q, k, v, z, xbc, dt_raw = jnp.split(proj, IN_SPLITS, axis=-1)
    att, k_rows, v_rows = attn_mixer(q, k, v, pos, q_norm_g, k_norm_g, win_k, win_v)
    ssd, ssd_buf_new, ssd_h = ssd_mixer(z, xbc, dt_raw, ssd_buf, ssd_h0, ssd_conv_w, ssd_conv_b,
                                        ssd_dt_bias, ssd_a_log, ssd_d, ssd_norm_g)
    mix = jnp.concatenate([att, ssd], axis=-1).astype(x.dtype)
    x = (x + mix @ w_out).astype(x.dtype)
    h = rms_norm(x, norm2_g) @ w_up
    h, ffn_buf_new = causal_dwconv(h, ffn_buf, ffn_conv_w, ffn_conv_b)
    gate, up = jnp.split(h, 2, axis=-1)
    x = (x + (jax.nn.silu(gate) * up) @ w_down).astype(x.dtype)
    return x, (k_rows, v_rows, ssd_buf_new, ssd_h, ffn_buf_new)


def setup_inputs(seed: int = 0) -> dict:
    key = jax.random.key(seed)
    ks = jax.random.split(key, 24)
    f32 = jnp.float32
    win_len = min(MAX_WINDOW, PAST_LEN)

    def nrm(k, shape, scale):
        return jax.random.normal(k, shape, f32) * scale

    dt0 = jnp.exp(jax.random.uniform(ks[13], (DEPTH, H_S), f32, math.log(1e-3), math.log(1e-1)))
    return {
        'x_prompt': nrm(ks[0], (BATCH, SEQ, D_MODEL), 1.0),
        'x_sample': nrm(ks[1], (DEC_BATCH, DEC_SEQ, D_MODEL), 1.0),
        'cache_win_k': nrm(ks[2], (DEPTH, DEC_BATCH, win_len, H_A, HD), 1.0),
        'cache_win_v': nrm(ks[3], (DEPTH, DEC_BATCH, win_len, H_A, HD), 1.0),
        'state_ssd_conv': nrm(ks[4], (DEPTH, DEC_BATCH, SSD_CONV - 1, D_XBC), 1.0),
        'state_ssd': nrm(ks[5], (DEPTH, DEC_BATCH, H_S, SSD_HEAD_DIM, SSD_STATE), 0.1),
        'state_ffn_conv': nrm(ks[6], (DEPTH, DEC_BATCH, FFN_CONV - 1, 2 * D_FF), 1.0),
        'norm1_g': 1.0 + nrm(ks[7], (DEPTH, D_MODEL), 0.02),
        'w_in': nrm(ks[8], (DEPTH, D_MODEL, D_IN_PROJ), D_MODEL ** -0.5),
        'q_norm_g': 1.0 + nrm(ks[9], (DEPTH, HD), 0.02),
        'k_norm_g': 1.0 + nrm(ks[10], (DEPTH, HD), 0.02),
        'ssd_conv_w': nrm(ks[11], (DEPTH, SSD_CONV, D_XBC), SSD_CONV ** -0.5),
        'ssd_conv_b': nrm(ks[12], (DEPTH, D_XBC), 0.02),
        'ssd_dt_bias': dt0 + jnp.log(-jnp.expm1(-dt0)),
        'ssd_a_log': jnp.log(jax.random.uniform(ks[14], (DEPTH, H_S), f32, 1.0, 16.0)),
        'ssd_d': 1.0 + nrm(ks[15], (DEPTH, H_S), 0.1),
        'ssd_norm_g': 1.0 + nrm(ks[16], (DEPTH, D_SSD), 0.02),
        'w_out': nrm(ks[17], (DEPTH, D_MIX, D_MODEL), D_MIX ** -0.5),
        'norm2_g': 1.0 + nrm(ks[18], (DEPTH, D_MODEL), 0.02),
        'w_up': nrm(ks[19], (DEPTH, D_MODEL, 2 * D_FF), D_MODEL ** -0.5),
        'ffn_conv_w': nrm(ks[20], (DEPTH, FFN_CONV, 2 * D_FF), 0.2).at[:, -1].add(1.0),
        'ffn_conv_b': nrm(ks[21], (DEPTH, 2 * D_FF), 0.02),
        'w_down': nrm(ks[22], (DEPTH, D_FF, D_MODEL), D_FF ** -0.5),
    }


def reference(x_prompt, x_sample, cache_win_k, cache_win_v, state_ssd_conv, state_ssd, state_ffn_conv,
              norm1_g, w_in, q_norm_g, k_norm_g, ssd_conv_w, ssd_conv_b, ssd_dt_bias, ssd_a_log,
              ssd_d, ssd_norm_g, w_out, norm2_g, w_up, ffn_conv_w, ffn_conv_b, w_down):
    bp, lp, _ = x_prompt.shape
    pos_p = jnp.arange(lp)
    pos_s = PAST_LEN + jnp.arange(x_sample.shape[1])
    zero_ssd_buf = jnp.zeros((bp, SSD_CONV - 1, D_XBC), x_prompt.dtype)
    zero_ssd_h = jnp.zeros((bp, H_S, SSD_HEAD_DIM, SSD_STATE), jnp.float32)
    zero_ffn_buf = jnp.zeros((bp, FFN_CONV - 1, 2 * D_FF), x_prompt.dtype)
    yp, ys = x_prompt, x_sample
    st_p, st_s = [], []
    for i in range(DEPTH):
        w_i = (norm1_g[i], w_in[i], q_norm_g[i], k_norm_g[i], ssd_conv_w[i], ssd_conv_b[i],
               ssd_dt_bias[i], ssd_a_log[i], ssd_d[i], ssd_norm_g[i], w_out[i],
               norm2_g[i], w_up[i], ffn_conv_w[i], ffn_conv_b[i], w_down[i])
        yp, sp = trunk_layer(yp, pos_p, None, None, zero_ssd_buf, zero_ssd_h, zero_ffn_buf, *w_i)
        ys, ss = trunk_layer(ys, pos_s, cache_win_k[i], cache_win_v[i], state_ssd_conv[i],
                             state_ssd[i], state_ffn_conv[i], *w_i)
        st_p.append(sp)
        st_s.append(ss)
    win_k_prompt = jnp.stack([s[0] for s in st_p])
    win_v_prompt = jnp.stack([s[1] for s in st_p])
    ssd_conv_prompt = jnp.stack([s[2] for s in st_p])
    ssd_state_prompt = jnp.stack([s[3] for s in st_p])
    ffn_conv_prompt = jnp.stack([s[4] for s in st_p])
    win_k_sample = jnp.stack([s[0] for s in st_s])
    win_v_sample = jnp.stack([s[1] for s in st_s])
    ssd_conv_sample = jnp.stack([s[2] for s in st_s])
    ssd_state_sample = jnp.stack([s[3] for s in st_s])
    ffn_conv_sample = jnp.stack([s[4] for s in st_s])
    return (yp, ys, win_k_prompt, win_v_prompt, win_k_sample, win_v_sample,
            ssd_conv_prompt, ssd_conv_sample, ssd_state_prompt, ssd_state_sample,
            ffn_conv_prompt, ffn_conv_sample)
```

```python
import functools

import numpy as np
import jax
import jax.numpy as jnp
from jax import lax
from jax.experimental import pallas as pl
from jax.experimental.pallas import tpu as pltpu

f32 = jnp.float32
bf16 = jnp.bfloat16

D_MODEL = 2048
BATCH = 4
SEQ = 2048
DEPTH = 4
DEC_BATCH = 8
DEC_SEQ = 4
PAST_LEN = 16384
HD = 64
D_ATT = 1024
H_A = 16
DILATIONS = (1, 4, 16)
WIN_KEYS = 128
MAX_WINDOW = 2048
ROPE_THETA = 10000.0
D_SSD = 1024
H_S = 16
SSD_STATE = 128
SSD_GROUPS = 4
SSD_CONV = 4
D_XBC = 2048
D_IN_PROJ = 6160
D_IN_PAD = 6400
D_FF = 5632
FFN_CONV = 3
EPS = 1e-6

LANES = 128
SUBLANES = 8
CHUNK = 128
S_STEPS = SUBLANES
S_ROWS = S_STEPS * DEC_BATCH

COL_Z = 3 * D_ATT
COL_XBC = COL_Z + D_SSD
COL_DT = COL_XBC + D_XBC

VMEM_LIMIT = 56 * 1024 * 1024


def _cparams(sem):
    return pltpu.CompilerParams(dimension_semantics=sem, vmem_limit_bytes=VMEM_LIMIT)


def _dot(a, b):
    return jnp.dot(a, b, preferred_element_type=f32)


def _dot_nt(a, b):
    return lax.dot_general(a, b, (((1,), (1,)), ((), ())), preferred_element_type=f32)


def _split(x, parts):
    out = []
    rem = x
    for p in range(parts):
        hi = rem.astype(bf16)
        out.append(hi)
        if p + 1 < parts:
            rem = rem - hi.astype(f32)
    return out


def _split_dot(x, w, parts):
    acc = None
    for hi in _split(x, parts):
        t = _dot(hi, w)
        acc = t if acc is None else acc + t
    return acc


def _silu(x):
    return x * (1.0 / (1.0 + jnp.exp(-x)))


def _softplus(x):
    return jnp.maximum(x, 0.0) + jnp.log1p(jnp.exp(-jnp.abs(x)))


def _rmsnorm_to(x_ref, g_ref, xn_ref, rows):
    chunk = min(rows, 256)

    def body(i, c):
        r = pl.multiple_of(i * chunk, chunk)
        x = x_ref[pl.ds(r, chunk), :]
        ms = jnp.mean(x * x, axis=-1, keepdims=True)
        xn_ref[pl.ds(r, chunk), :] = (x * lax.rsqrt(ms + EPS) * g_ref[...]).astype(bf16)
        return c

    lax.fori_loop(0, rows // chunk, body, 0)


def _inproj_kernel(x_ref, g_ref, w_ref, o_ref, xn_ref, *, tm):
    @pl.when(pl.program_id(1) == 0)
    def _():
        _rmsnorm_to(x_ref, g_ref, xn_ref, tm)

    o_ref[...] = _dot(xn_ref[...], w_ref[...])


def _in_proj(x, g, w, layer, tm, tn=1280):
    rows = x.shape[0]
    return pl.pallas_call(
        functools.partial(_inproj_kernel, tm=tm),
        grid=(rows // tm, D_IN_PAD // tn),
        in_specs=[
            pl.BlockSpec((tm, D_MODEL), lambda i, j: (i, 0)),
            pl.BlockSpec((None, 1, D_MODEL), lambda i, j: (layer, 0, 0)),
            pl.BlockSpec((None, D_MODEL, tn), lambda i, j: (layer, 0, j)),
        ],
        out_specs=pl.BlockSpec((tm, tn), lambda i, j: (i, j)),
        out_shape=jax.ShapeDtypeStruct((rows, D_IN_PAD), f32),
        scratch_shapes=[pltpu.VMEM((tm, D_MODEL), bf16)],
        compiler_params=_cparams(("arbitrary", "arbitrary")),
        name="in_proj",
    )(x, g, w)


def _outproj_kernel(a_ref, s_ref, wa_ref, ws_ref, x_ref, o_ref):
    acc = _dot(a_ref[...].astype(bf16), wa_ref[...])
    acc = acc + _dot(s_ref[...].astype(bf16), ws_ref[...])
    o_ref[...] = x_ref[...] + acc


def _out_proj(att, ssd, w, x, layer, tm, tn=512):
    rows = x.shape[0]
    return pl.pallas_call(
        _outproj_kernel,
        grid=(rows // tm, D_MODEL // tn),
        in_specs=[
            pl.BlockSpec((tm, D_ATT), lambda i, j: (i, 0)),
            pl.BlockSpec((tm, D_SSD), lambda i, j: (i, 0)),
            pl.BlockSpec((None, D_ATT, tn), lambda i, j: (layer, 0, j)),
            pl.BlockSpec((None, D_SSD, tn), lambda i, j: (layer, 1, j)),
            pl.BlockSpec((tm, tn), lambda i, j: (i, j)),
        ],
        out_specs=pl.BlockSpec((tm, tn), lambda i, j: (i, j)),
        out_shape=jax.ShapeDtypeStruct((rows, D_MODEL), f32),
        compiler_params=_cparams(("arbitrary", "arbitrary")),
        name="out_proj",
    )(att, ssd, w, w, x)


FFN_PAD = 2 * SUBLANES


def _ffn_kernel(x_ref, g_ref, wg_ref, wu_ref, cwg_ref, cwu_ref, cbg_ref, cbu_ref, wd_ref,
                cig_ref, ciu_ref, o_ref, hsg_ref, hsu_ref,
                xn_ref, acc_ref, hbg_ref, hbu_ref, cag_ref, cau_ref, act_ref,
                *, tm, tiles_per_seq, per_tile, rc):
    i = pl.program_id(0)
    j = pl.program_id(1)
    nj = pl.num_programs(1)
    pad = FFN_PAD

    @pl.when(j == 0)
    def _():
        _rmsnorm_to(x_ref, g_ref, xn_ref, tm)
        acc_ref[...] = jnp.zeros_like(acc_ref)

    first = (i % tiles_per_seq) == 0
    for w_ref, ci_ref, hb_ref, ca_ref, hs_ref in (
            (wg_ref, cig_ref, hbg_ref, cag_ref, hsg_ref),
            (wu_ref, ciu_ref, hbu_ref, cau_ref, hsu_ref)):
        if per_tile:
            hb_ref[...] = ci_ref[...]
            h = _dot(xn_ref[...], w_ref[...])
            for b in range(tm // SUBLANES):
                hb_ref[b * pad + SUBLANES:(b + 1) * pad, :] = h[b * SUBLANES:(b + 1) * SUBLANES, :]
            hs_ref[...] = hb_ref[...]
        else:
            @pl.when(first)
            def _():
                hb_ref[0:pad, :] = ci_ref[...]

            @pl.when(jnp.logical_not(first))
            def _():
                hb_ref[0:pad, :] = ca_ref[j]

            hb_ref[pad:pad + tm, :] = _dot(xn_ref[...], w_ref[...])
            ca_ref[j] = hb_ref[tm:tm + pad, :]
            hs_ref[...] = hb_ref[tm:tm + pad, :]

    for c in range(tm // rc):
        r = (c * pad + SUBLANES) if per_tile else (pad + c * rc)
        convs = []
        for hb_ref, cw_ref, cb_ref in ((hbg_ref, cwg_ref, cbg_ref), (hbu_ref, cwu_ref, cbu_ref)):
            y = cb_ref[...] + hb_ref[r - 2:r - 2 + rc, :] * cw_ref[0:1, :]
            y = y + hb_ref[r - 1:r - 1 + rc, :] * cw_ref[1:2, :]
            y = y + hb_ref[r:r + rc, :] * cw_ref[2:3, :]
            convs.append(y)
        act_ref[c * rc:(c + 1) * rc, :] = (_silu(convs[0]) * convs[1]).astype(bf16)

    acc_ref[...] += _dot(act_ref[...], wd_ref[...])

    @pl.when(j == nj - 1)
    def _():
        o_ref[...] = x_ref[...] + acc_ref[...]


def _ffn(x, g, w_up, cw, cb, w_down, cin, layer, *, tm, tiles_per_seq, per_tile, tf=512):
    rows = x.shape[0]
    nj = D_FF // tf
    pad = FFN_PAD
    rc = SUBLANES if per_tile else min(tm, 64)
    hb_rows = (tm // SUBLANES) * pad if per_tile else tm + pad
    st_rows = hb_rows if per_tile else pad
    kern = functools.partial(_ffn_kernel, tm=tm, tiles_per_seq=tiles_per_seq, per_tile=per_tile, rc=rc)
    return pl.pallas_call(
        kern,
        grid=(rows // tm, nj),
        in_specs=[
            pl.BlockSpec((tm, D_MODEL), lambda i, j: (i, 0)),
            pl.BlockSpec((None, 1, D_MODEL), lambda i, j: (layer, 0, 0)),
            pl.BlockSpec((None, D_MODEL, tf), lambda i, j: (layer, 0, j)),
            pl.BlockSpec((None, D_MODEL, tf), lambda i, j: (layer, 0, nj + j)),
            pl.BlockSpec((None, FFN_CONV, tf), lambda i, j: (layer, 0, j)),
            pl.BlockSpec((None, FFN_CONV, tf), lambda i, j: (layer, 0, nj + j)),
            pl.BlockSpec((None, 1, tf), lambda i, j: (layer, 0, j)),
            pl.BlockSpec((None, 1, tf), lambda i, j: (layer, 0, nj + j)),
            pl.BlockSpec((None, tf, D_MODEL), lambda i, j: (layer, j, 0)),
            pl.BlockSpec((None, st_rows, tf), lambda i, j: (i // tiles_per_seq, 0, j)),
            pl.BlockSpec((None, st_rows, tf), lambda i, j: (i // tiles_per_seq, 0, nj + j)),
        ],
        out_specs=[
            pl.BlockSpec((tm, D_MODEL), lambda i, j: (i, 0)),
            pl.BlockSpec((None, st_rows, tf), lambda i, j: (i, 0, j)),
            pl.BlockSpec((None, st_rows, tf), lambda i, j: (i, 0, j)),
        ],
        out_shape=[
            jax.ShapeDtypeStruct((rows, D_MODEL), f32),
            jax.ShapeDtypeStruct((rows // tm, st_rows, D_FF), f32),
            jax.ShapeDtypeStruct((rows // tm, st_rows, D_FF), f32),
        ],
        scratch_shapes=[
            pltpu.VMEM((tm, D_MODEL), bf16),
            pltpu.VMEM((tm, D_MODEL), f32),
            pltpu.VMEM((hb_rows, tf), f32),
            pltpu.VMEM((hb_rows, tf), f32),
            pltpu.VMEM((nj, pad, tf), f32),
            pltpu.VMEM((nj, pad, tf), f32),
            pltpu.VMEM((tm, tf), bf16),
        ],
        compiler_params=_cparams(("arbitrary", "arbitrary")),
        name="conv_ffn",
    )(x, g, w_up, w_up, cw, cw, cb, cb, w_down, cin, cin)


def _head_norm_rope(x, gain, cos, sin, e2, low_half):
    ssq = _split_dot(x * x, e2, 2)
    xn = x * lax.rsqrt(ssq * (1.0 / HD) + EPS) * gain
    rot = jnp.where(low_half, pltpu.roll(xn, LANES - HD // 2, 1), pltpu.roll(xn, HD // 2, 1))
    return xn * cos + rot * sin


def _attn_block(qb, kb, vb, valid, lane_lo):
    zero = jnp.zeros_like(qb)
    q2 = jnp.concatenate([jnp.where(lane_lo, qb, zero), jnp.where(lane_lo, zero, qb)], axis=0)
    s = _dot_nt(q2, kb)
    s = jnp.where(valid, s, -jnp.inf)
    m = jnp.max(s, axis=1, keepdims=True)
    p = jnp.exp(s - m)
    l = jnp.sum(p, axis=1, keepdims=True)
    o2 = _dot(p.astype(bf16), vb)
    shape = (CHUNK, LANES)
    o = jnp.where(lane_lo, o2[:CHUNK], o2[CHUNK:])
    me = jnp.where(lane_lo, jnp.broadcast_to(m[:CHUNK], shape), jnp.broadcast_to(m[CHUNK:], shape))
    le = jnp.where(lane_lo, jnp.broadcast_to(l[:CHUNK], shape), jnp.broadcast_to(l[CHUNK:], shape))
    return o, me, le


def _attn_kernel(q_ref, k_ref, v_ref, cos_ref, sin_ref, qg_ref, kg_ref, e2_ref,
                 att_ref, ko_ref, vo_ref,
                 qf_ref, q1_ref, k1_ref, v1_ref, q4_ref, k4_ref, v4_ref, q16_ref, k16_ref, v16_ref,
                 o_ref, m_ref, l_ref):
    lane = lax.broadcasted_iota(jnp.int32, (CHUNK, LANES), 1)
    lane_lo = lane < HD
    rows_a = 256
    lane_a = lax.broadcasted_iota(jnp.int32, (rows_a, LANES), 1)
    low_half_a = (lane_a % HD) < (HD // 2)

    def prep(i, c):
        r = pl.multiple_of(i * rows_a, rows_a)
        sl = pl.ds(r, rows_a)
        cos = cos_ref[sl, :]
        sin = sin_ref[sl, :]
        q = _head_norm_rope(q_ref[sl, :], qg_ref[...], cos, sin, e2_ref[...], low_half_a) * (HD ** -0.5)
        k = _head_norm_rope(k_ref[sl, :], kg_ref[...], cos, sin, e2_ref[...], low_half_a)
        v = v_ref[sl, :]
        qf_ref[sl, :] = q
        ko_ref[sl, :] = k
        vo_ref[sl, :] = v
        q1_ref[sl, :] = q.astype(bf16)
        k1_ref[sl, :] = k.astype(bf16)
        v1_ref[sl, :] = v.astype(bf16)
        return c

    lax.fori_loop(0, SEQ // rows_a, prep, 0)

    for d, qd, kd, vd in ((4, q4_ref, k4_ref, v4_ref), (16, q16_ref, k16_ref, v16_ref)):
        mlen = SEQ // d
        for r in range(d):
            src = pl.ds(r, mlen, stride=d)
            dst = pl.ds(r * mlen, mlen)
            qd[dst, :] = qf_ref[src, :].astype(bf16)
            kd[dst, :] = ko_ref[src, :].astype(bf16)
            vd[dst, :] = v_ref[src, :].astype(bf16)

    qi = lax.broadcasted_iota(jnp.int32, (2 * CHUNK, CHUNK), 0) % CHUNK
    ki = lax.broadcasted_iota(jnp.int32, (2 * CHUNK, CHUNK), 1)
    valid_first = ki <= qi
    qi2 = lax.broadcasted_iota(jnp.int32, (2 * CHUNK, 2 * CHUNK), 0) % CHUNK
    ki2 = lax.broadcasted_iota(jnp.int32, (2 * CHUNK, 2 * CHUNK), 1)
    valid_band = (ki2 >= qi2) & (ki2 <= qi2 + CHUNK)

    def store_first(dst, o, me, le):
        o_ref[dst, :] = o
        m_ref[dst, :] = me
        l_ref[dst, :] = le

    def merge(dst, o, me, le):
        mo = m_ref[dst, :]
        mn = jnp.maximum(mo, me)
        a_old = jnp.exp(mo - mn)
        a_new = jnp.exp(me - mn)
        o_ref[dst, :] = o_ref[dst, :] * a_old + o * a_new
        l_ref[dst, :] = l_ref[dst, :] * a_old + le * a_new
        m_ref[dst, :] = mn

    for d, qd, kd, vd in ((1, q1_ref, k1_ref, v1_ref), (4, q4_ref, k4_ref, v4_ref),
                          (16, q16_ref, k16_ref, v16_ref)):
        mlen = SEQ // d
        nblk = mlen // CHUNK
        combine = store_first if d == 1 else merge
        for r in range(d):
            base = r * mlen
            sl0 = pl.ds(base, CHUNK)
            o, me, le = _attn_block(qd[sl0, :], kd[sl0, :], vd[sl0, :], valid_first, lane_lo)
            combine(pl.ds(r, CHUNK, stride=d) if d > 1 else sl0, o, me, le)
            if nblk > 1:
                def band(c, carry, base=base, r=r, d=d, qd=qd, kd=kd, vd=vd, combine=combine):
                    q0 = pl.multiple_of(base + c * CHUNK, CHUNK)
                    k0 = pl.multiple_of(base + (c - 1) * CHUNK, CHUNK)
                    o, me, le = _attn_block(qd[pl.ds(q0, CHUNK), :], kd[pl.ds(k0, 2 * CHUNK), :],
                                            vd[pl.ds(k0, 2 * CHUNK), :], valid_band, lane_lo)
                    if d > 1:
                        dst = pl.ds(r + d * CHUNK * c, CHUNK, stride=d)
                    else:
                        dst = pl.ds(q0, CHUNK)
                    combine(dst, o, me, le)
                    return carry

                lax.fori_loop(1, nblk, band, 0)

    def fin(i, c):
        r = pl.multiple_of(i * rows_a, rows_a)
        sl = pl.ds(r, rows_a)
        att_ref[sl, :] = (o_ref[sl, :] / l_ref[sl, :]).astype(bf16)
        return c

    lax.fori_loop(0, SEQ // rows_a, fin, 0)


def _attention_prompt(proj3, cos, sin, qg, kg, e2, layer):
    nhp = D_ATT // LANES
    blk = lambda off: pl.BlockSpec((None, SEQ, LANES), lambda b, h: (b, 0, off + h))
    tab = pl.BlockSpec((SEQ, LANES), lambda b, h: (0, 0))
    gain = pl.BlockSpec((None, 1, LANES), lambda b, h: (layer, 0, 0))
    bscr = lambda: pltpu.VMEM((SEQ, LANES), bf16)
    fscr = lambda: pltpu.VMEM((SEQ, LANES), f32)
    return pl.pallas_call(
        _attn_kernel,
        grid=(BATCH, nhp),
        in_specs=[blk(0), blk(nhp), blk(2 * nhp), tab, tab, gain, gain,
                  pl.BlockSpec((LANES, LANES), lambda b, h: (0, 0))],
        out_specs=[pl.BlockSpec((None, SEQ, LANES), lambda b, h: (b, 0, h))] * 3,
        out_shape=[jax.ShapeDtypeStruct((BATCH, SEQ, D_ATT), bf16),
                   jax.ShapeDtypeStruct((BATCH, SEQ, D_ATT), f32),
                   jax.ShapeDtypeStruct((BATCH, SEQ, D_ATT), f32)],
        scratch_shapes=[fscr()] + [bscr() for _ in range(9)] + [fscr(), fscr(), fscr()],
        compiler_params=_cparams(("arbitrary", "arbitrary")),
        name="attn_prompt",
    )(proj3, proj3, proj3, cos, sin, qg, kg, e2)


GRP = 4 * HD
KPAD = MAX_WINDOW + LANES


def _attn_sample_kernel(q_ref, k_ref, v_ref, ck_ref, cv_ref, cos_ref, sin_ref, qg_ref, kg_ref,
                        e2_ref, w_ref, att_ref, kn_ref, kc_ref, vc_ref):
    lane = lax.broadcasted_iota(jnp.int32, (S_STEPS, LANES), 1)
    low_half = (lane % HD) < (HD // 2)
    e2 = e2_ref[...]
    qs, ks = [], []
    for t in range(GRP // LANES):
        cs = slice(t * LANES, (t + 1) * LANES)
        cos = cos_ref[:, cs]
        sin = sin_ref[:, cs]
        qs.append(_head_norm_rope(q_ref[:, cs], qg_ref[...], cos, sin, e2, low_half) * (HD ** -0.5))
        ks.append(_head_norm_rope(k_ref[:, cs], kg_ref[...], cos, sin, e2, low_half))
    q = jnp.concatenate(qs, axis=1)
    k = jnp.concatenate(ks, axis=1)
    kn_ref[...] = k

    tail = KPAD - MAX_WINDOW
    kc_ref[0:MAX_WINDOW, :] = ck_ref[...].astype(bf16)
    vc_ref[0:MAX_WINDOW, :] = cv_ref[...].astype(bf16)
    kc_ref[MAX_WINDOW:KPAD, :] = jnp.zeros((tail, GRP), bf16)
    vc_ref[MAX_WINDOW:KPAD, :] = jnp.zeros((tail, GRP), bf16)
    kc_ref[MAX_WINDOW:MAX_WINDOW + 2 * S_STEPS, :] = jnp.concatenate(
        [k, jnp.zeros_like(k)], axis=0).astype(bf16)
    vc_ref[MAX_WINDOW:MAX_WINDOW + 2 * S_STEPS, :] = jnp.concatenate(
        [v_ref[...], jnp.zeros_like(k)], axis=0).astype(bf16)

    nq = (GRP // HD) * S_STEPS
    qt = jnp.concatenate([q] * (GRP // HD) + [jnp.zeros((LANES - nq, GRP), f32)], axis=0)
    row_h = lax.broadcasted_iota(jnp.int32, (LANES, GRP), 0) // S_STEPS
    lane_h = lax.broadcasted_iota(jnp.int32, (LANES, GRP), 1) // HD
    same_head = row_h == lane_h
    qt = jnp.where(same_head, qt, 0.0).astype(bf16)
    s = _dot_nt(kc_ref[...], qt)
    w = w_ref[...]
    keep = w > 0.0
    m = jnp.max(jnp.where(keep, s, -jnp.inf), axis=0, keepdims=True)
    e = jnp.where(keep, w * jnp.exp(s - m), 0.0)
    den = jnp.sum(e, axis=0, keepdims=True)
    pt = jnp.transpose(e / den).astype(bf16)
    res = _dot(pt, vc_ref[...])
    res = jnp.where(same_head, res, 0.0)
    out = res[0:S_STEPS]
    for h in range(1, GRP // HD):
        out = out + res[h * S_STEPS:(h + 1) * S_STEPS]
    att_ref[...] = out


def _attention_sample(proj, ck, cv, cos, sin, qg, kg, e2, wmask, layer):
    ng = D_ATT // GRP
    col = lambda off: pl.BlockSpec((S_STEPS, GRP), lambda b, g: (b, off + g))
    cache = pl.BlockSpec((None, None, MAX_WINDOW, GRP), lambda b, g: (layer, b, 0, g))
    tab = pl.BlockSpec((S_STEPS, GRP), lambda b, g: (0, 0))
    gain = pl.BlockSpec((None, 1, LANES), lambda b, g: (layer, 0, 0))
    return pl.pallas_call(
        _attn_sample_kernel,
        grid=(DEC_BATCH, ng),
        in_specs=[col(0), col(ng), col(2 * ng), cache, cache, tab, tab, gain, gain,
                  pl.BlockSpec((LANES, LANES), lambda b, g: (0, 0)),
                  pl.BlockSpec((KPAD, LANES), lambda b, g: (0, 0))],
        out_specs=[pl.BlockSpec((S_STEPS, GRP), lambda b, g: (b, g))] * 2,
        out_shape=[jax.ShapeDtypeStruct((S_ROWS, D_ATT), f32)] * 2,
        scratch_shapes=[pltpu.VMEM((KPAD, GRP), bf16), pltpu.VMEM((KPAD, GRP), bf16)],
        compiler_params=_cparams(("arbitrary", "arbitrary")),
        name="attn_sample",
    )(proj, proj, proj, ck, cv, cos, sin, qg, kg, e2, wmask)


def _sample_key_weights():
    w = np.zeros((KPAD, LANES), np.float32)
    for t in range(S_STEPS):
        mult = np.zeros((KPAD,), np.float32)
        if t < DEC_SEQ:
            dist = np.full((KPAD,), -1, np.int64)
            dist[:MAX_WINDOW] = MAX_WINDOW + t - np.arange(MAX_WINDOW)
            dist[MAX_WINDOW:MAX_WINDOW + DEC_SEQ] = t - np.arange(DEC_SEQ)
            for d in DILATIONS:
                mult += ((dist >= 0) & (dist % d == 0) & (dist <= WIN_KEYS * d)).astype(np.float32)
        else:
            mult[MAX_WINDOW + t] = 1.0
        for h in range(GRP // HD):
            w[:, h * S_STEPS + t] = mult
    w[0, (GRP // HD) * S_STEPS:] = 1.0
    return w


def _ssd_kernel(z_ref, xbc_ref, dt_ref, ci_ref, h0_ref, cw_ref, cb_ref, dtb_ref, a_ref, dsk_ref,
                ng_ref, tri_ref, exp_ref, y_ref, ho_ref,
                cbuf_ref, xc_ref, st_ref, zb_ref, db_ref, yb_ref,
                *, padded, valid, has_h0):
    c = pl.program_id(1)
    nc = pl.num_programs(1)

    if padded:
        zb_ref[...] = jnp.zeros_like(zb_ref)
        db_ref[...] = jnp.zeros_like(db_ref)
        cbuf_ref[SUBLANES:, :] = jnp.zeros((CHUNK, D_XBC), f32)
        zb_ref[0:S_STEPS, :] = z_ref[...]
        db_ref[0:S_STEPS, :] = dt_ref[...]
        cbuf_ref[SUBLANES:SUBLANES + S_STEPS, :] = xbc_ref[...]
        zsrc, dsrc = zb_ref, db_ref
    else:
        cbuf_ref[SUBLANES:, :] = xbc_ref[...]
        zsrc, dsrc = z_ref, dt_ref

    @pl.when(c == 0)
    def _():
        cbuf_ref[0:SUBLANES, :] = ci_ref[...]
        if has_h0:
            st_ref[...] = jnp.transpose(h0_ref[...])
        else:
            st_ref[...] = jnp.zeros_like(st_ref)

    for t in range(D_XBC // 256):
        cs = slice(t * 256, (t + 1) * 256)
        acc = cb_ref[:, cs] + cbuf_ref[5:5 + CHUNK, cs] * cw_ref[0:1, cs]
        acc = acc + cbuf_ref[6:6 + CHUNK, cs] * cw_ref[1:2, cs]
        acc = acc + cbuf_ref[7:7 + CHUNK, cs] * cw_ref[2:3, cs]
        acc = acc + cbuf_ref[8:8 + CHUNK, cs] * cw_ref[3:4, cs]
        xc_ref[:, cs] = _silu(acc)
    cbuf_ref[0:SUBLANES, :] = cbuf_ref[CHUNK:CHUNK + SUBLANES, :]

    tri = tri_ref[...]
    expand = exp_ref[...]
    dt = _softplus(dsrc[...] + dtb_ref[...])
    a = dt * a_ref[...]
    a_cum = None
    for hi in _split(a, 3):
        t_ = _dot(tri, hi)
        a_cum = t_ if a_cum is None else a_cum + t_
    a_cum_t = jnp.transpose(a_cum)
    dt_e = _split_dot(dt, expand, 2)
    acum_e = _split_dot(a_cum, expand, 3)
    alast_e = acum_e[valid - 1:valid, :]
    row = lax.broadcasted_iota(jnp.int32, (CHUNK, D_SSD), 0)

    xs = xc_ref[:, 0:D_SSD]
    xdt = xs * dt_e
    xdt_b = xdt.astype(bf16)
    xend_b = jnp.where(row < valid, xdt * jnp.exp(alast_e - acum_e), 0.0).astype(bf16)

    ii = lax.broadcasted_iota(jnp.int32, (CHUNK, CHUNK), 0)
    jj = lax.broadcasted_iota(jnp.int32, (CHUNK, CHUNK), 1)
    causal = jj <= ii
    lane_lo = lax.broadcasted_iota(jnp.int32, (CHUNK, LANES), 1) < HD
    hpg = H_S // SSD_GROUPS
    gw = hpg * HD
    for g in range(SSD_GROUPS):
        gs = slice(g * gw, (g + 1) * gw)
        b0 = D_SSD + g * SSD_STATE
        c0 = D_SSD + (SSD_GROUPS + g) * SSD_STATE
        bm = xc_ref[:, b0:b0 + SSD_STATE]
        cm = xc_ref[:, c0:c0 + SSD_STATE].astype(bf16)
        cbm = _dot_nt(cm, bm.astype(bf16))
        bt = jnp.transpose(bm).astype(bf16)
        st_new = _dot(bt, xend_b[:, gs])
        y_off = _dot(cm, st_ref[:, gs].astype(bf16))
        for pr in range(hpg // 2):
            ys = []
            ps = slice(g * gw + pr * LANES, g * gw + (pr + 1) * LANES)
            for hh in range(2):
                h = g * hpg + pr * 2 + hh
                seg = a_cum[:, h:h + 1] - a_cum_t[h:h + 1, :]
                gm = (cbm * jnp.exp(jnp.where(causal, seg, -jnp.inf))).astype(bf16)
                ys.append(_dot(gm, xdt_b[:, ps]))
            yb_ref[:, ps] = jnp.where(lane_lo, ys[0], ys[1])
        yb_ref[:, gs] = yb_ref[:, gs] + y_off * jnp.exp(acum_e[:, gs])
        st_ref[:, gs] = jnp.exp(alast_e[:, gs]) * st_ref[:, gs] + st_new

    y = yb_ref[...] + dsk_ref[...] * xs
    y = y * _silu(zsrc[...])
    ms = jnp.mean(y * y, axis=-1, keepdims=True)
    y = y * lax.rsqrt(ms + EPS) * ng_ref[...]
    if padded:
        y_ref[...] = y[0:S_STEPS]
    else:
        y_ref[...] = y.astype(y_ref.dtype)

    @pl.when(c == nc - 1)
    def _():
        ho_ref[...] = jnp.transpose(st_ref[...])


def _ssd(proj, cinit, h0, cw, cb, dtb, a_neg, dskip, ng, tri, expand, layer, *, sample):
    if sample:
        nb, nc, rows = DEC_BATCH, 1, S_STEPS
        y_dtype = f32
        h0_spec = pl.BlockSpec((None, None, D_SSD, SSD_STATE), lambda b, c: (layer, b, 0, 0))
    else:
        nb, nc, rows = BATCH, SEQ // CHUNK, CHUNK
        y_dtype = bf16
        h0_spec = pl.BlockSpec((None, None, D_SSD, SSD_STATE), lambda b, c: (0, 0, 0, 0))
    rowblk = lambda width, colblk: pl.BlockSpec((rows, width), lambda b, c: (b * nc + c, colblk))
    vec = lambda width: pl.BlockSpec((None, 1, width), lambda b, c: (layer, 0, 0))
    const = lambda shape: pl.BlockSpec(shape, lambda b, c: (0, 0))
    kern = functools.partial(_ssd_kernel, padded=sample, valid=DEC_SEQ if sample else CHUNK,
                             has_h0=sample)
    return pl.pallas_call(
        kern,
        grid=(nb, nc),
        in_specs=[
            rowblk(D_SSD, COL_Z // D_SSD),
            rowblk(D_XBC, COL_XBC // D_XBC),
            rowblk(LANES, COL_DT // LANES),
            pl.BlockSpec((None, SUBLANES, D_XBC), lambda b, c: (b, 0, 0)),
            h0_spec,
            pl.BlockSpec((None, SSD_CONV, D_XBC), lambda b, c: (layer, 0, 0)),
            vec(D_XBC), vec(LANES), vec(LANES), vec(D_SSD), vec(D_SSD),
            const((CHUNK, CHUNK)), const((LANES, D_SSD)),
        ],
        out_specs=[rowblk(D_SSD, 0), pl.BlockSpec((None, D_SSD, SSD_STATE), lambda b, c: (b, 0, 0))],
        out_shape=[jax.ShapeDtypeStruct((nb * nc * rows, D_SSD), y_dtype),
                   jax.ShapeDtypeStruct((nb, D_SSD, SSD_STATE), f32)],
        scratch_shapes=[
            pltpu.VMEM((CHUNK + SUBLANES, D_XBC), f32),
            pltpu.VMEM((CHUNK, D_XBC), f32),
            pltpu.VMEM((SSD_STATE, D_SSD), f32),
            pltpu.VMEM((CHUNK, D_SSD), f32),
            pltpu.VMEM((CHUNK, LANES), f32),
            pltpu.VMEM((CHUNK, D_SSD), f32),
        ],
        compiler_params=_cparams(("arbitrary", "arbitrary")),
        name="ssd_sample" if sample else "ssd_prompt",
    )(proj, proj, proj, cinit, h0, cw, cb, dtb, a_neg, dskip, ng, tri, expand)


def _rope_tables(pos, width):
    half = HD // 2
    inv = ROPE_THETA ** (-jnp.arange(half, dtype=f32) / half)
    ang = pos.astype(f32)[:, None] * inv[None, :]
    cos = jnp.cos(ang)
    sin = jnp.sin(ang)
    cos_h = jnp.concatenate([cos, cos], axis=-1)
    sin_h = jnp.concatenate([-sin, sin], axis=-1)
    reps = width // HD
    return jnp.tile(cos_h, (1, reps)), jnp.tile(sin_h, (1, reps))


def kernel(x_prompt, x_sample, cache_win_k, cache_win_v, state_ssd_conv, state_ssd, state_ffn_conv,
           norm1_g, w_in, q_norm_g, k_norm_g, ssd_conv_w, ssd_conv_b, ssd_dt_bias, ssd_a_log,
           ssd_d, ssd_norm_g, w_out, norm2_g, w_up, ffn_conv_w, ffn_conv_b, w_down):
    w_in_b = jnp.pad(w_in, ((0, 0), (0, 0), (0, D_IN_PAD - D_IN_PROJ))).astype(bf16)
    w_out_b = w_out.astype(bf16)
    w_up_b = w_up.astype(bf16)
    w_down_b = w_down.astype(bf16)
    g1 = norm1_g[:, None, :]
    g2 = norm2_g[:, None, :]
    qg = jnp.tile(q_norm_g, (1, LANES // HD))[:, None, :]
    kg = jnp.tile(k_norm_g, (1, LANES // HD))[:, None, :]
    cb_ssd = ssd_conv_b[:, None, :]
    lane_pad = ((0, 0), (0, LANES - H_S))
    dtb = jnp.pad(ssd_dt_bias, lane_pad)[:, None, :]
    a_neg = jnp.pad(-jnp.exp(ssd_a_log.astype(f32)), lane_pad)[:, None, :]
    dskip = jnp.repeat(ssd_d, HD, axis=1)[:, None, :]
    ng = ssd_norm_g[:, None, :]
    cb_ffn = ffn_conv_b[:, None, :]

    idx = np.arange(LANES)
    e2 = jnp.asarray((idx[:, None] // HD == idx[None, :] // HD).astype(np.float32), dtype=bf16)
    tri = jnp.asarray((idx[None, :] <= idx[:, None]).astype(np.float32), dtype=bf16)
    expand = jnp.asarray((idx[:, None] == np.arange(D_SSD)[None, :] // HD).astype(np.float32), dtype=bf16)
    wmask = jnp.asarray(_sample_key_weights())
    cos_p, sin_p = _rope_tables(jnp.arange(SEQ), LANES)
    cos_s, sin_s = _rope_tables(PAST_LEN + jnp.arange(S_STEPS), GRP)

    xp = x_prompt.reshape(BATCH * SEQ, D_MODEL)
    xs = jnp.pad(x_sample, ((0, 0), (0, S_STEPS - DEC_SEQ), (0, 0))).reshape(S_ROWS, D_MODEL)

    ck = cache_win_k.reshape(DEPTH, DEC_BATCH, MAX_WINDOW, D_ATT)
    cv = cache_win_v.reshape(DEPTH, DEC_BATCH, MAX_WINDOW, D_ATT)
    ssd_ci_p = jnp.zeros((BATCH, SUBLANES, D_XBC), f32)
    ssd_ci_s = jnp.pad(state_ssd_conv, ((0, 0), (0, 0), (SUBLANES - (SSD_CONV - 1), 0), (0, 0)))
    h0_s = state_ssd.reshape(DEPTH, DEC_BATCH, D_SSD, SSD_STATE)
    h0_p = jnp.zeros((1, 1, D_SSD, SSD_STATE), f32)
    ffn_ci_p = jnp.zeros((BATCH, FFN_PAD, 2 * D_FF), f32)
    ffn_ci_s = jnp.pad(state_ffn_conv, ((0, 0), (0, 0), (SUBLANES - (FFN_CONV - 1), S_STEPS), (0, 0))
                       ).reshape(DEPTH, 1, DEC_BATCH * FFN_PAD, 2 * D_FF)

    tm_p = 512
    outs = {k: [] for k in ("kp", "vp", "ks", "vs", "cp", "cs", "hp", "hs", "fp", "fs")}
    for i in range(DEPTH):
        proj = _in_proj(xp, g1, w_in_b, i, tm_p)
        proj3 = proj.reshape(BATCH, SEQ, D_IN_PAD)
        att, k_rows, v_rows = _attention_prompt(proj3, cos_p, sin_p, qg, kg, e2, i)
        ssd, h_last = _ssd(proj, ssd_ci_p, h0_p, ssd_conv_w, cb_ssd, dtb, a_neg, dskip, ng, tri,
                           expand, i, sample=False)
        x1 = _out_proj(att.reshape(BATCH * SEQ, D_ATT), ssd, w_out_b, xp, i, tm_p)
        xp, hs_g, hs_u = _ffn(x1, g2, w_up_b, ffn_conv_w, cb_ffn, w_down_b, ffn_ci_p, i, tm=tm_p,
                              tiles_per_seq=SEQ // tm_p, per_tile=False)
        outs["kp"].append(k_rows.reshape(BATCH, SEQ, H_A, HD))
        outs["vp"].append(v_rows.reshape(BATCH, SEQ, H_A, HD))
        outs["cp"].append(proj3[:, SEQ - (SSD_CONV - 1):, COL_XBC:COL_XBC + D_XBC])
        outs["hp"].append(h_last.reshape(BATCH, H_S, HD, SSD_STATE))
        last = slice(SEQ // tm_p - 1, None, SEQ // tm_p)
        hs = jnp.concatenate([hs_g[last], hs_u[last]], axis=-1)
        outs["fp"].append(hs[:, FFN_PAD - (FFN_CONV - 1):, :])

        proj_s = _in_proj(xs, g1, w_in_b, i, S_ROWS)
        att_s, kn_s = _attention_sample(proj_s, ck, cv, cos_s, sin_s, qg, kg, e2, wmask, i)
        ssd_s, h_last_s = _ssd(proj_s, ssd_ci_s[i], h0_s, ssd_conv_w, cb_ssd, dtb, a_neg, dskip, ng,
                               tri, expand, i, sample=True)
        x1_s = _out_proj(att_s, ssd_s, w_out_b, xs, i, S_ROWS)
        xs, hs_g, hs_u = _ffn(x1_s, g2, w_up_b, ffn_conv_w, cb_ffn, w_down_b, ffn_ci_s[i], i,
                              tm=S_ROWS, tiles_per_seq=1, per_tile=True)
        p3 = proj_s.reshape(DEC_BATCH, S_STEPS, D_IN_PAD)
        outs["ks"].append(kn_s.reshape(DEC_BATCH, S_STEPS, H_A, HD)[:, :DEC_SEQ])
        outs["vs"].append(p3[:, :DEC_SEQ, 2 * D_ATT:3 * D_ATT].reshape(DEC_BATCH, DEC_SEQ, H_A, HD))
        outs["cs"].append(p3[:, DEC_SEQ - (SSD_CONV - 1):DEC_SEQ, COL_XBC:COL_XBC + D_XBC])
        outs["hs"].append(h_last_s.reshape(DEC_BATCH, H_S, HD, SSD_STATE))
        hs = jnp.concatenate([hs_g[0], hs_u[0]], axis=-1).reshape(DEC_BATCH, FFN_PAD, 2 * D_FF)
        lo = SUBLANES + DEC_SEQ - (FFN_CONV - 1)
        outs["fs"].append(hs[:, lo:lo + FFN_CONV - 1, :])

    yp = xp.reshape(BATCH, SEQ, D_MODEL)
    ys = xs.reshape(DEC_BATCH, S_STEPS, D_MODEL)[:, :DEC_SEQ]
    st = lambda k: jnp.stack(outs[k])
    return (yp, ys, st("kp"), st("vp"), st("ks"), st("vs"), st("cp"), st("cs"),
            st("hp"), st("hs"), st("fp"), st("fs"))
```

```python
import functools

import numpy as np
import jax
import jax.numpy as jnp
from jax import lax
from jax.experimental import pallas as pl
from jax.experimental.pallas import tpu as pltpu

f32 = jnp.float32
bf16 = jnp.bfloat16

D_MODEL = 2048
BATCH = 4
SEQ = 2048
DEPTH = 4
DEC_BATCH = 8
DEC_SEQ = 4
PAST_LEN = 16384
HD = 64
D_ATT = 1024
H_A = 16
DILATIONS = (1, 4, 16)
WIN_KEYS = 128
MAX_WINDOW = 2048
ROPE_THETA = 10000.0
D_SSD = 1024
H_S = 16
SSD_STATE = 128
SSD_GROUPS = 4
SSD_CONV = 4
D_XBC = 2048
D_FF = 5632
FFN_CONV = 3
EPS = 1e-6

LANES = 128
SUBLANES = 8
CHUNK = 128
S_STEPS = SUBLANES
S_ROWS = S_STEPS * DEC_BATCH

COL_Z = 3 * D_ATT
COL_XBC = COL_Z + D_SSD
D_PROJ = COL_XBC + D_XBC

VMEM_LIMIT = 56 * 1024 * 1024


def _cparams(sem):
    return pltpu.CompilerParams(dimension_semantics=sem, vmem_limit_bytes=VMEM_LIMIT)


def _dot(a, b):
    return jnp.dot(a, b, preferred_element_type=f32)


def _dot_nt(a, b):
    return lax.dot_general(a, b, (((1,), (1,)), ((), ())), preferred_element_type=f32)


def _split(x, parts):
    out = []
    rem = x
    for p in range(parts):
        hi = rem.astype(bf16)
        out.append(hi)
        if p + 1 < parts:
            rem = rem - hi.astype(f32)
    return out


def _split_dot(x, w, parts):
    acc = None
    for hi in _split(x, parts):
        t = _dot(hi, w)
        acc = t if acc is None else acc + t
    return acc


def _silu(x):
    return x * (1.0 / (1.0 + jnp.exp(-x)))


def _softplus(x):
    return jnp.maximum(x, 0.0) + jnp.log1p(jnp.exp(-jnp.abs(x)))


def _rmsnorm_to(x_ref, g_ref, xn_ref, rows):
    chunk = min(rows, 256)

    def body(i, c):
        r = pl.multiple_of(i * chunk, chunk)
        x = x_ref[pl.ds(r, chunk), :]
        ms = jnp.mean(x * x, axis=-1, keepdims=True)
        xn_ref[pl.ds(r, chunk), :] = (x * lax.rsqrt(ms + EPS) * g_ref[...]).astype(bf16)
        return c

    lax.fori_loop(0, rows // chunk, body, 0)


def _inproj_kernel(x_ref, g_ref, w_ref, wdt_ref, o_ref, dt_ref, xn_ref, *, tm):
    @pl.when(pl.program_id(1) == 0)
    def _():
        _rmsnorm_to(x_ref, g_ref, xn_ref, tm)
        dt_ref[...] = _dot(xn_ref[...], wdt_ref[...])

    o_ref[...] = _dot(xn_ref[...], w_ref[...])


def _in_proj(x, g, w, wdt, layer, tm, tn=1536):
    rows = x.shape[0]
    return pl.pallas_call(
        functools.partial(_inproj_kernel, tm=tm),
        grid=(rows // tm, D_PROJ // tn),
        in_specs=[
            pl.BlockSpec((tm, D_MODEL), lambda i, j: (i, 0)),
            pl.BlockSpec((None, 1, D_MODEL), lambda i, j: (layer, 0, 0)),
            pl.BlockSpec((None, D_MODEL, tn), lambda i, j: (layer, 0, j)),
            pl.BlockSpec((None, D_MODEL, LANES), lambda i, j: (layer, 0, 0)),
        ],
        out_specs=[pl.BlockSpec((tm, tn), lambda i, j: (i, j)),
                   pl.BlockSpec((tm, LANES), lambda i, j: (i, 0))],
        out_shape=[jax.ShapeDtypeStruct((rows, D_PROJ), f32),
                   jax.ShapeDtypeStruct((rows, LANES), f32)],
        scratch_shapes=[pltpu.VMEM((tm, D_MODEL), bf16)],
        compiler_params=_cparams(("arbitrary", "arbitrary")),
        name="in_proj",
    )(x, g, w, wdt)


def _outproj_kernel(a_ref, s_ref, wa_ref, ws_ref, x_ref, o_ref):
    acc = _dot(a_ref[...].astype(bf16), wa_ref[...])
    acc = acc + _dot(s_ref[...].astype(bf16), ws_ref[...])
    o_ref[...] = x_ref[...] + acc


def _out_proj(att, ssd, w, x, layer, tm, tn=D_MODEL):
    rows = x.shape[0]
    return pl.pallas_call(
        _outproj_kernel,
        grid=(rows // tm, D_MODEL // tn),
        in_specs=[
            pl.BlockSpec((tm, D_ATT), lambda i, j: (i, 0)),
            pl.BlockSpec((tm, D_SSD), lambda i, j: (i, 0)),
            pl.BlockSpec((None, D_ATT, tn), lambda i, j: (layer, 0, j)),
            pl.BlockSpec((None, D_SSD, tn), lambda i, j: (layer, 1, j)),
            pl.BlockSpec((tm, tn), lambda i, j: (i, j)),
        ],
        out_specs=pl.BlockSpec((tm, tn), lambda i, j: (i, j)),
        out_shape=jax.ShapeDtypeStruct((rows, D_MODEL), f32),
        compiler_params=_cparams(("arbitrary", "arbitrary")),
        name="out_proj",
    )(att, ssd, w, w, x)


FFN_PAD = 2 * SUBLANES


def _ffn_kernel(x_ref, g_ref, wg_ref, wu_ref, cwg_ref, cwu_ref, cbg_ref, cbu_ref, wd_ref,
                cig_ref, ciu_ref, o_ref, hsg_ref, hsu_ref,
                xn_ref, acc_ref, hbg_ref, hbu_ref, cag_ref, cau_ref, act_ref,
                *, tm, tiles_per_seq, per_tile, rc, parts):
    i = pl.program_id(0)
    j = pl.program_id(1)
    nj = pl.num_programs(1)
    pad = FFN_PAD
    gate = (wg_ref, cig_ref, hbg_ref, cag_ref, hsg_ref, cwg_ref, cbg_ref)
    up = (wu_ref, ciu_ref, hbu_ref, cau_ref, hsu_ref, cwu_ref, cbu_ref)

    @pl.when(j == 0)
    def _():
        _rmsnorm_to(x_ref, g_ref, xn_ref, tm)
        acc_ref[...] = jnp.zeros_like(acc_ref)

    def conv_act(r, out_rows):
        convs = []
        for _, _, hb_ref, _, _, cw_ref, cb_ref in (gate, up):
            y = cb_ref[...] + hb_ref[r - 2:r - 2 + rc, :] * cw_ref[0:1, :]
            y = y + hb_ref[r - 1:r - 1 + rc, :] * cw_ref[1:2, :]
            y = y + hb_ref[r:r + rc, :] * cw_ref[2:3, :]
            convs.append(y)
        act_ref[out_rows, :] = (_silu(convs[0]) * convs[1]).astype(bf16)

    if per_tile:
        for w_ref, ci_ref, hb_ref, _, hs_ref, _, _ in (gate, up):
            hb_ref[...] = ci_ref[...]
            h = _dot(xn_ref[...], w_ref[...])
            for b in range(tm // SUBLANES):
                hb_ref[b * pad + SUBLANES:(b + 1) * pad, :] = h[b * SUBLANES:(b + 1) * SUBLANES, :]
            hs_ref[...] = hb_ref[...]
        for b in range(tm // SUBLANES):
            conv_act(b * pad + SUBLANES, slice(b * SUBLANES, (b + 1) * SUBLANES))
        acc_ref[...] += _dot(act_ref[...], wd_ref[...])
    else:
        first = (i % tiles_per_seq) == 0
        for _, ci_ref, hb_ref, ca_ref, _, _, _ in (gate, up):
            @pl.when(first)
            def _():
                hb_ref[0:pad, :] = ci_ref[...]

            @pl.when(jnp.logical_not(first))
            def _():
                hb_ref[0:pad, :] = ca_ref[j]

        pr = tm // parts
        for p in range(parts):
            rows = slice(p * pr, (p + 1) * pr)
            for w_ref, _, hb_ref, _, _, _, _ in (gate, up):
                hb_ref[pad + p * pr:pad + (p + 1) * pr, :] = _dot(xn_ref[rows, :], w_ref[...])
            for c in range(pr // rc):
                r0 = p * pr + c * rc
                conv_act(pad + r0, slice(r0, r0 + rc))
            acc_ref[rows, :] += _dot(act_ref[rows, :], wd_ref[...])
        for _, _, hb_ref, ca_ref, hs_ref, _, _ in (gate, up):
            ca_ref[j] = hb_ref[tm:tm + pad, :]
            hs_ref[...] = hb_ref[tm:tm + pad, :]

    @pl.when(j == nj - 1)
    def _():
        o_ref[...] = x_ref[...] + acc_ref[...]


def _ffn(x, g, w_up, cw, cb, w_down, cin, layer, *, tm, tiles_per_seq, per_tile, tf=512):
    rows = x.shape[0]
    nj = D_FF // tf
    pad = FFN_PAD
    rc = SUBLANES if per_tile else min(tm, 64)
    hb_rows = (tm // SUBLANES) * pad if per_tile else tm + pad
    st_rows = hb_rows if per_tile else pad
    kern = functools.partial(_ffn_kernel, tm=tm, tiles_per_seq=tiles_per_seq, per_tile=per_tile, rc=rc,
                             parts=1 if per_tile else 2)
    return pl.pallas_call(
        kern,
        grid=(rows // tm, nj),
        in_specs=[
            pl.BlockSpec((tm, D_MODEL), lambda i, j: (i, 0)),
            pl.BlockSpec((None, 1, D_MODEL), lambda i, j: (layer, 0, 0)),
            pl.BlockSpec((None, D_MODEL, tf), lambda i, j: (layer, 0, j)),
            pl.BlockSpec((None, D_MODEL, tf), lambda i, j: (layer, 0, nj + j)),
            pl.BlockSpec((None, FFN_CONV, tf), lambda i, j: (layer, 0, j)),
            pl.BlockSpec((None, FFN_CONV, tf), lambda i, j: (layer, 0, nj + j)),
            pl.BlockSpec((None, 1, tf), lambda i, j: (layer, 0, j)),
            pl.BlockSpec((None, 1, tf), lambda i, j: (layer, 0, nj + j)),
            pl.BlockSpec((None, tf, D_MODEL), lambda i, j: (layer, j, 0)),
            pl.BlockSpec((None, st_rows, tf), lambda i, j: (i // tiles_per_seq, 0, j)),
            pl.BlockSpec((None, st_rows, tf), lambda i, j: (i // tiles_per_seq, 0, nj + j)),
        ],
        out_specs=[
            pl.BlockSpec((tm, D_MODEL), lambda i, j: (i, 0)),
            pl.BlockSpec((None, st_rows, tf), lambda i, j: (i, 0, j)),
            pl.BlockSpec((None, st_rows, tf), lambda i, j: (i, 0, j)),
        ],
        out_shape=[
            jax.ShapeDtypeStruct((rows, D_MODEL), f32),
            jax.ShapeDtypeStruct((rows // tm, st_rows, D_FF), f32),
            jax.ShapeDtypeStruct((rows // tm, st_rows, D_FF), f32),
        ],
        scratch_shapes=[
            pltpu.VMEM((tm, D_MODEL), bf16),
            pltpu.VMEM((tm, D_MODEL), f32),
            pltpu.VMEM((hb_rows, tf), f32),
            pltpu.VMEM((hb_rows, tf), f32),
            pltpu.VMEM((nj, pad, tf), f32),
            pltpu.VMEM((nj, pad, tf), f32),
            pltpu.VMEM((tm, tf), bf16),
        ],
        compiler_params=_cparams(("arbitrary", "arbitrary")),
        name="conv_ffn",
    )(x, g, w_up, w_up, cw, cw, cb, cb, w_down, cin, cin)


def _head_norm_rope(x, gain, cos, sin, e2, low_half):
    ssq = _split_dot(x * x, e2, 2)
    xn = x * lax.rsqrt(ssq * (1.0 / HD) + EPS) * gain
    rot = jnp.where(low_half, pltpu.roll(xn, LANES - HD // 2, 1), pltpu.roll(xn, HD // 2, 1))
    return xn * cos + rot * sin


def _attn_block(q_ref, k_ref, ve_ref, q_rows, k_rows, bias_ref, lane_lo):
    qb = q_ref[q_rows, :]
    zero = jnp.zeros_like(qb)
    q2 = jnp.concatenate([jnp.where(lane_lo, qb, zero), jnp.where(lane_lo, zero, qb)], axis=0)
    s = _dot_nt(q2, k_ref[k_rows, :]) + bias_ref[...]
    m = jnp.max(s, axis=1, keepdims=True)
    p = jnp.exp(s - m).astype(bf16)
    r = _dot(p, ve_ref[k_rows, :])
    shape = (CHUNK, LANES)
    o = jnp.where(lane_lo, r[:CHUNK, :LANES], r[CHUNK:, :LANES])
    le = jnp.where(lane_lo, r[:CHUNK, LANES:], r[CHUNK:, LANES:])
    me = jnp.where(lane_lo, jnp.broadcast_to(m[:CHUNK], shape), jnp.broadcast_to(m[CHUNK:], shape))
    return o, me, le


def _attn_kernel(*refs, aliased):
    (q_ref, k_ref, v_ref, cos_ref, sin_ref, qg_ref, kg_ref, e2_ref, bband_ref, bfirst_ref) = refs[:10]
    refs = refs[12:] if aliased else refs[10:]
    (att_ref, ko_ref, vo_ref, qf_ref, q4f_ref, k4f_ref, v4f_ref,
     q1_ref, k1_ref, v1_ref, q4_ref, k4_ref, v4_ref, q16_ref, k16_ref, v16_ref,
     o1_ref, m1_ref, l1_ref, o4_ref, m4_ref, l4_ref, o16_ref, m16_ref, l16_ref) = refs
    lane_lo = lax.broadcasted_iota(jnp.int32, (CHUNK, LANES), 1) < HD
    rows_a = 256
    lane_a = lax.broadcasted_iota(jnp.int32, (rows_a, LANES), 1)
    low_half_a = (lane_a % HD) < (HD // 2)

    @pl.when((pl.program_id(0) == 0) & (pl.program_id(1) == 0))
    def _():
        ones = jnp.ones((SEQ, LANES), bf16)
        v1_ref[:, LANES:] = ones
        v4_ref[:, LANES:] = ones
        v16_ref[:, LANES:] = ones

    def prep(i, c):
        r = pl.multiple_of(i * rows_a, rows_a)
        sl = pl.ds(r, rows_a)
        cos = cos_ref[sl, :]
        sin = sin_ref[sl, :]
        q = _head_norm_rope(q_ref[sl, :], qg_ref[...], cos, sin, e2_ref[...], low_half_a) * (HD ** -0.5)
        k = _head_norm_rope(k_ref[sl, :], kg_ref[...], cos, sin, e2_ref[...], low_half_a)
        v = v_ref[sl, :]
        qf_ref[sl, :] = q
        ko_ref[sl, :] = k
        vo_ref[sl, :] = v
        q1_ref[sl, :] = q.astype(bf16)
        k1_ref[sl, :] = k.astype(bf16)
        v1_ref[sl, 0:LANES] = v.astype(bf16)
        return c

    lax.fori_loop(0, SEQ // rows_a, prep, 0, unroll=2)

    quarter = SEQ // 4
    streams = ((qf_ref, q4f_ref, q4_ref, q16_ref), (ko_ref, k4f_ref, k4_ref, k16_ref),
               (v_ref, v4f_ref, v4_ref, v16_ref))
    for r in range(4):
        src = pl.ds(r, quarter, stride=4)
        dst = pl.ds(r * quarter, quarter)
        for tok_ref, d4f_ref, d4_ref, _ in streams:
            x = tok_ref[src, :]
            d4f_ref[dst, :] = x
            d4_ref[dst, 0:LANES] = x.astype(bf16)
    for r16 in range(16):
        src = pl.ds((r16 % 4) * quarter + r16 // 4, CHUNK, stride=4)
        dst = pl.ds(r16 * CHUNK, CHUNK)
        for _, d4f_ref, _, d16_ref in streams:
            d16_ref[dst, 0:LANES] = d4f_ref[src, :].astype(bf16)

    def first_block(qd, kd, vd, outs, base, dst):
        rows = pl.ds(base, CHUNK)
        res = _attn_block(qd, kd, vd, rows, rows, bfirst_ref, lane_lo)
        for ref, val in zip(outs, res):
            ref[dst, :] = val

    def band_block(qd, kd, vd, outs, q0, dst):
        k_rows = pl.ds(pl.multiple_of(q0 - CHUNK, CHUNK), 2 * CHUNK)
        res = _attn_block(qd, kd, vd, pl.ds(q0, CHUNK), k_rows, bband_ref, lane_lo)
        for ref, val in zip(outs, res):
            ref[dst, :] = val

    p1 = (q1_ref, k1_ref, v1_ref, (o1_ref, m1_ref, l1_ref))
    p4 = (q4_ref, k4_ref, v4_ref, (o4_ref, m4_ref, l4_ref))
    p16 = (q16_ref, k16_ref, v16_ref, (o16_ref, m16_ref, l16_ref))

    first_block(*p1, 0, pl.ds(0, CHUNK))
    for r in range(4):
        first_block(*p4, r * quarter, pl.ds(r * quarter, CHUNK))

    n1 = 5

    def band1(it, carry):
        for u in range(n1):
            q0 = pl.multiple_of((1 + it * n1 + u) * CHUNK, CHUNK)
            band_block(*p1, q0, pl.ds(q0, CHUNK))
        return carry

    lax.fori_loop(0, (SEQ // CHUNK - 1) // n1, band1, 0)

    def band4(it, carry):
        for u in range(6):
            q0 = pl.multiple_of((2 * it + u // 3) * quarter + (1 + u % 3) * CHUNK, CHUNK)
            band_block(*p4, q0, pl.ds(q0, CHUNK))
        return carry

    lax.fori_loop(0, 2, band4, 0)

    def first16(it, carry):
        for u in range(8):
            base = pl.multiple_of((8 * it + u) * CHUNK, CHUNK)
            dst = pl.ds((u % 4) * quarter + 2 * it + u // 4, CHUNK, stride=4)
            first_block(*p16, base, dst)
        return carry

    lax.fori_loop(0, 2, first16, 0)

    for r in range(4):
        def fin(mb, c, r=r):
            sl = pl.ds(pl.multiple_of(r * quarter + mb * CHUNK, CHUNK), CHUNK)
            tok = pl.ds(r + 4 * CHUNK * mb, CHUNK, stride=4)
            m1, m4, m16 = m1_ref[tok, :], m4_ref[sl, :], m16_ref[sl, :]
            mx = jnp.maximum(jnp.maximum(m1, m4), m16)
            a1, a4, a16 = jnp.exp(m1 - mx), jnp.exp(m4 - mx), jnp.exp(m16 - mx)
            num = o1_ref[tok, :] * a1 + o4_ref[sl, :] * a4 + o16_ref[sl, :] * a16
            den = l1_ref[tok, :] * a1 + l4_ref[sl, :] * a4 + l16_ref[sl, :] * a16
            att_ref[tok, :] = num / den
            return c

        lax.fori_loop(0, quarter // CHUNK, fin, 0, unroll=2)


def _attention_prompt(proj3, cos, sin, qg, kg, e2, bias_band, bias_first, layer, k_all, v_all):
    nhp = D_ATT // LANES
    blk = lambda off: pl.BlockSpec((None, SEQ, LANES), lambda b, h: (b, 0, off + h))
    tab = pl.BlockSpec((SEQ, LANES), lambda b, h: (0, 0))
    gain = pl.BlockSpec((None, 1, LANES), lambda b, h: (layer, 0, 0))
    const = lambda shape: pl.BlockSpec(shape, lambda b, h: (0, 0))
    stacked = pl.BlockSpec((None, None, SEQ, LANES), lambda b, h: (layer, b, 0, h))
    bscr = lambda w: pltpu.VMEM((SEQ, w), bf16)
    fscr = lambda: pltpu.VMEM((SEQ, LANES), f32)
    aliased = k_all is not None
    in_specs = [blk(0), blk(nhp), blk(2 * nhp), tab, tab, gain, gain, const((LANES, LANES)),
                const((2 * CHUNK, 2 * CHUNK)), const((2 * CHUNK, CHUNK))]
    args = [proj3, proj3, proj3, cos, sin, qg, kg, e2, bias_band, bias_first]
    aliases = {}
    if aliased:
        in_specs += [pl.BlockSpec(memory_space=pl.ANY)] * 2
        aliases = {len(args): 1, len(args) + 1: 2}
        args += [k_all, v_all]
    return pl.pallas_call(
        functools.partial(_attn_kernel, aliased=aliased),
        grid=(BATCH, nhp),
        in_specs=in_specs,
        out_specs=[pl.BlockSpec((None, SEQ, LANES), lambda b, h: (b, 0, h)), stacked, stacked],
        out_shape=[jax.ShapeDtypeStruct((BATCH, SEQ, D_ATT), f32),
                   jax.ShapeDtypeStruct((DEPTH, BATCH, SEQ, D_ATT), f32),
                   jax.ShapeDtypeStruct((DEPTH, BATCH, SEQ, D_ATT), f32)],
        scratch_shapes=([fscr() for _ in range(4)] + [bscr(LANES), bscr(LANES), bscr(2 * LANES)] * 3
                        + [fscr() for _ in range(9)]),
        input_output_aliases=aliases,
        compiler_params=_cparams(("arbitrary", "arbitrary")),
        name="attn_prompt",
    )(*args)


def _attn_biases():
    qi = (np.arange(2 * CHUNK) % CHUNK)[:, None]
    ki = np.arange(2 * CHUNK)[None, :]
    band = np.where((ki >= qi) & (ki <= qi + CHUNK), 0.0, -np.inf).astype(np.float32)
    first = np.where(ki[:, :CHUNK] <= qi, 0.0, -np.inf).astype(np.float32)
    return band, first


GRP = 4 * HD
KPAD = MAX_WINDOW + LANES


def _attn_sample_kernel(q_ref, k_ref, v_ref, ck_ref, cv_ref, cos_ref, sin_ref, qg_ref, kg_ref,
                        e2_ref, w_ref, att_ref, kn_ref, kc_ref, vc_ref):
    lane = lax.broadcasted_iota(jnp.int32, (S_STEPS, LANES), 1)
    low_half = (lane % HD) < (HD // 2)
    e2 = e2_ref[...]
    qs, ks = [], []
    for t in range(GRP // LANES):
        cs = slice(t * LANES, (t + 1) * LANES)
        cos = cos_ref[:, cs]
        sin = sin_ref[:, cs]
        qs.append(_head_norm_rope(q_ref[:, cs], qg_ref[...], cos, sin, e2, low_half) * (HD ** -0.5))
        ks.append(_head_norm_rope(k_ref[:, cs], kg_ref[...], cos, sin, e2, low_half))
    q = jnp.concatenate(qs, axis=1)
    k = jnp.concatenate(ks, axis=1)
    kn_ref[...] = k

    tail = KPAD - MAX_WINDOW
    kc_ref[0:MAX_WINDOW, :] = ck_ref[...].astype(bf16)
    vc_ref[0:MAX_WINDOW, :] = cv_ref[...].astype(bf16)
    kc_ref[MAX_WINDOW:KPAD, :] = jnp.zeros((tail, GRP), bf16)
    vc_ref[MAX_WINDOW:KPAD, :] = jnp.zeros((tail, GRP), bf16)
    kc_ref[MAX_WINDOW:MAX_WINDOW + 2 * S_STEPS, :] = jnp.concatenate(
        [k, jnp.zeros_like(k)], axis=0).astype(bf16)
    vc_ref[MAX_WINDOW:MAX_WINDOW + 2 * S_STEPS, :] = jnp.concatenate(
        [v_ref[...], jnp.zeros_like(k)], axis=0).astype(bf16)

    nq = (GRP // HD) * S_STEPS
    qt = jnp.concatenate([q] * (GRP // HD) + [jnp.zeros((LANES - nq, GRP), f32)], axis=0)
    row_h = lax.broadcasted_iota(jnp.int32, (LANES, GRP), 0) // S_STEPS
    lane_h = lax.broadcasted_iota(jnp.int32, (LANES, GRP), 1) // HD
    same_head = row_h == lane_h
    qt = jnp.where(same_head, qt, 0.0).astype(bf16)
    s = _dot_nt(kc_ref[...], qt)
    w = w_ref[...]
    keep = w > 0.0
    m = jnp.max(jnp.where(keep, s, -jnp.inf), axis=0, keepdims=True)
    e = jnp.where(keep, w * jnp.exp(s - m), 0.0)
    den = jnp.sum(e, axis=0, keepdims=True)
    pt = jnp.transpose(e / den).astype(bf16)
    res = _dot(pt, vc_ref[...])
    res = jnp.where(same_head, res, 0.0)
    out = res[0:S_STEPS]
    for h in range(1, GRP // HD):
        out = out + res[h * S_STEPS:(h + 1) * S_STEPS]
    att_ref[...] = out


def _attention_sample(proj, ck, cv, cos, sin, qg, kg, e2, wmask, layer):
    ng = D_ATT // GRP
    col = lambda off: pl.BlockSpec((S_STEPS, GRP), lambda b, g: (b, off + g))
    cache = pl.BlockSpec((None, None, MAX_WINDOW, GRP), lambda b, g: (layer, b, 0, g))
    tab = pl.BlockSpec((S_STEPS, GRP), lambda b, g: (0, 0))
    gain = pl.BlockSpec((None, 1, LANES), lambda b, g: (layer, 0, 0))
    return pl.pallas_call(
        _attn_sample_kernel,
        grid=(DEC_BATCH, ng),
        in_specs=[col(0), col(ng), col(2 * ng), cache, cache, tab, tab, gain, gain,
                  pl.BlockSpec((LANES, LANES), lambda b, g: (0, 0)),
                  pl.BlockSpec((KPAD, LANES), lambda b, g: (0, 0))],
        out_specs=[pl.BlockSpec((S_STEPS, GRP), lambda b, g: (b, g))] * 2,
        out_shape=[jax.ShapeDtypeStruct((S_ROWS, D_ATT), f32)] * 2,
        scratch_shapes=[pltpu.VMEM((KPAD, GRP), bf16), pltpu.VMEM((KPAD, GRP), bf16)],
        compiler_params=_cparams(("arbitrary", "arbitrary")),
        name="attn_sample",
    )(proj, proj, proj, ck, cv, cos, sin, qg, kg, e2, wmask)


def _sample_key_weights():
    w = np.zeros((KPAD, LANES), np.float32)
    for t in range(S_STEPS):
        mult = np.zeros((KPAD,), np.float32)
        if t < DEC_SEQ:
            dist = np.full((KPAD,), -1, np.int64)
            dist[:MAX_WINDOW] = MAX_WINDOW + t - np.arange(MAX_WINDOW)
            dist[MAX_WINDOW:MAX_WINDOW + DEC_SEQ] = t - np.arange(DEC_SEQ)
            for d in DILATIONS:
                mult += ((dist >= 0) & (dist % d == 0) & (dist <= WIN_KEYS * d)).astype(np.float32)
        else:
            mult[MAX_WINDOW + t] = 1.0
        for h in range(GRP // HD):
            w[:, h * S_STEPS + t] = mult
    w[0, (GRP // HD) * S_STEPS:] = 1.0
    return w


def _ssd_kernel(z_ref, xbc_ref, dt_ref, ci_ref, h0_ref, cw_ref, cb_ref, dtb_ref, a_ref, dsk_ref,
                ng_ref, tri_ref, exp_ref, y_ref, ho_ref,
                cbuf_ref, xc_ref, st_ref, zb_ref, db_ref, yb_ref,
                *, padded, valid, has_h0):
    c = pl.program_id(1)
    nc = pl.num_programs(1)

    if padded:
        zb_ref[...] = jnp.zeros_like(zb_ref)
        db_ref[...] = jnp.zeros_like(db_ref)
        cbuf_ref[SUBLANES:, :] = jnp.zeros((CHUNK, D_XBC), f32)
        zb_ref[0:S_STEPS, :] = z_ref[...]
        db_ref[0:S_STEPS, :] = dt_ref[...]
        cbuf_ref[SUBLANES:SUBLANES + S_STEPS, :] = xbc_ref[...]
        zsrc, dsrc = zb_ref, db_ref
    else:
        cbuf_ref[SUBLANES:, :] = xbc_ref[...]
        zsrc, dsrc = z_ref, dt_ref

    @pl.when(c == 0)
    def _():
        cbuf_ref[0:SUBLANES, :] = ci_ref[...]
        if has_h0:
            st_ref[...] = jnp.transpose(h0_ref[...])
        else:
            st_ref[...] = jnp.zeros_like(st_ref)

    for t in range(D_XBC // 256):
        cs = slice(t * 256, (t + 1) * 256)
        acc = cb_ref[:, cs] + cbuf_ref[5:5 + CHUNK, cs] * cw_ref[0:1, cs]
        acc = acc + cbuf_ref[6:6 + CHUNK, cs] * cw_ref[1:2, cs]
        acc = acc + cbuf_ref[7:7 + CHUNK, cs] * cw_ref[2:3, cs]
        acc = acc + cbuf_ref[8:8 + CHUNK, cs] * cw_ref[3:4, cs]
        xc_ref[:, cs] = _silu(acc)
    cbuf_ref[0:SUBLANES, :] = cbuf_ref[CHUNK:CHUNK + SUBLANES, :]

    tri = tri_ref[...]
    expand = exp_ref[...]
    dt = _softplus(dsrc[...] + dtb_ref[...])
    a = dt * a_ref[...]
    a_cum = None
    for hi in _split(a, 3):
        t_ = _dot(tri, hi)
        a_cum = t_ if a_cum is None else a_cum + t_
    a_cum_t = jnp.transpose(a_cum)
    dt_e = _split_dot(dt, expand, 2)
    acum_e = _split_dot(a_cum, expand, 3)
    alast_e = acum_e[valid - 1:valid, :]
    row = lax.broadcasted_iota(jnp.int32, (CHUNK, D_SSD), 0)

    xs = xc_ref[:, 0:D_SSD]
    xdt = xs * dt_e
    xdt_b = xdt.astype(bf16)
    xend_b = jnp.where(row < valid, xdt * jnp.exp(alast_e - acum_e), 0.0).astype(bf16)

    ii = lax.broadcasted_iota(jnp.int32, (CHUNK, CHUNK), 0)
    jj = lax.broadcasted_iota(jnp.int32, (CHUNK, CHUNK), 1)
    causal = jj <= ii
    lane_lo = lax.broadcasted_iota(jnp.int32, (CHUNK, LANES), 1) < HD
    hpg = H_S // SSD_GROUPS
    gw = hpg * HD
    for g in range(SSD_GROUPS):
        gs = slice(g * gw, (g + 1) * gw)
        b0 = D_SSD + g * SSD_STATE
        c0 = D_SSD + (SSD_GROUPS + g) * SSD_STATE
        bm = xc_ref[:, b0:b0 + SSD_STATE]
        cm = xc_ref[:, c0:c0 + SSD_STATE].astype(bf16)
        cbm = _dot_nt(cm, bm.astype(bf16))
        bt = jnp.transpose(bm).astype(bf16)
        st_new = _dot(bt, xend_b[:, gs])
        y_off = _dot(cm, st_ref[:, gs].astype(bf16))
        for pr in range(hpg // 2):
            ys = []
            ps = slice(g * gw + pr * LANES, g * gw + (pr + 1) * LANES)
            for hh in range(2):
                h = g * hpg + pr * 2 + hh
                seg = a_cum[:, h:h + 1] - a_cum_t[h:h + 1, :]
                gm = (cbm * jnp.exp(jnp.where(causal, seg, -jnp.inf))).astype(bf16)
                ys.append(_dot(gm, xdt_b[:, ps]))
            yb_ref[:, ps] = jnp.where(lane_lo, ys[0], ys[1])
        yb_ref[:, gs] = yb_ref[:, gs] + y_off * jnp.exp(acum_e[:, gs])
        st_ref[:, gs] = jnp.exp(alast_e[:, gs]) * st_ref[:, gs] + st_new

    y = yb_ref[...] + dsk_ref[...] * xs
    y = y * _silu(zsrc[...])
    ms = jnp.mean(y * y, axis=-1, keepdims=True)
    y = y * lax.rsqrt(ms + EPS) * ng_ref[...]
    if padded:
        y_ref[...] = y[0:S_STEPS]
    else:
        y_ref[...] = y.astype(y_ref.dtype)

    @pl.when(c == nc - 1)
    def _():
        ho_ref[...] = jnp.transpose(st_ref[...])


def _ssd(proj, dt_raw, cinit, h0, cw, cb, dtb, a_neg, dskip, ng, tri, expand, layer, *, sample):
    if sample:
        nb, nc, rows = DEC_BATCH, 1, S_STEPS
        y_dtype = f32
        h0_spec = pl.BlockSpec((None, None, D_SSD, SSD_STATE), lambda b, c: (layer, b, 0, 0))
    else:
        nb, nc, rows = BATCH, SEQ // CHUNK, CHUNK
        y_dtype = bf16
        h0_spec = pl.BlockSpec((None, None, D_SSD, SSD_STATE), lambda b, c: (0, 0, 0, 0))
    rowblk = lambda width, colblk: pl.BlockSpec((rows, width), lambda b, c: (b * nc + c, colblk))
    vec = lambda width: pl.BlockSpec((None, 1, width), lambda b, c: (layer, 0, 0))
    const = lambda shape: pl.BlockSpec(shape, lambda b, c: (0, 0))
    kern = functools.partial(_ssd_kernel, padded=sample, valid=DEC_SEQ if sample else CHUNK,
                             has_h0=sample)
    return pl.pallas_call(
        kern,
        grid=(nb, nc),
        in_specs=[
            rowblk(D_SSD, COL_Z // D_SSD),
            rowblk(D_XBC, COL_XBC // D_XBC),
            rowblk(LANES, 0),
            pl.BlockSpec((None, SUBLANES, D_XBC), lambda b, c: (b, 0, 0)),
            h0_spec,
            pl.BlockSpec((None, SSD_CONV, D_XBC), lambda b, c: (layer, 0, 0)),
            vec(D_XBC), vec(LANES), vec(LANES), vec(D_SSD), vec(D_SSD),
            const((CHUNK, CHUNK)), const((LANES, D_SSD)),
        ],
        out_specs=[rowblk(D_SSD, 0), pl.BlockSpec((None, D_SSD, SSD_STATE), lambda b, c: (b, 0, 0))],
        out_shape=[jax.ShapeDtypeStruct((nb * nc * rows, D_SSD), y_dtype),
                   jax.ShapeDtypeStruct((nb, D_SSD, SSD_STATE), f32)],
        scratch_shapes=[
            pltpu.VMEM((CHUNK + SUBLANES, D_XBC), f32),
            pltpu.VMEM((CHUNK, D_XBC), f32),
            pltpu.VMEM((SSD_STATE, D_SSD), f32),
            pltpu.VMEM((CHUNK, D_SSD), f32),
            pltpu.VMEM((CHUNK, LANES), f32),
            pltpu.VMEM((CHUNK, D_SSD), f32),
        ],
        compiler_params=_cparams(("arbitrary", "arbitrary")),
        name="ssd_sample" if sample else "ssd_prompt",
    )(proj, proj, dt_raw, cinit, h0, cw, cb, dtb, a_neg, dskip, ng, tri, expand)


def _rope_tables(pos, width):
    half = HD // 2
    inv = ROPE_THETA ** (-jnp.arange(half, dtype=f32) / half)
    ang = pos.astype(f32)[:, None] * inv[None, :]
    cos = jnp.cos(ang)
    sin = jnp.sin(ang)
    cos_h = jnp.concatenate([cos, cos], axis=-1)
    sin_h = jnp.concatenate([-sin, sin], axis=-1)
    reps = width // HD
    return jnp.tile(cos_h, (1, reps)), jnp.tile(sin_h, (1, reps))


def kernel(x_prompt, x_sample, cache_win_k, cache_win_v, state_ssd_conv, state_ssd, state_ffn_conv,
           norm1_g, w_in, q_norm_g, k_norm_g, ssd_conv_w, ssd_conv_b, ssd_dt_bias, ssd_a_log,
           ssd_d, ssd_norm_g, w_out, norm2_g, w_up, ffn_conv_w, ffn_conv_b, w_down):
    w_in_b = w_in.astype(bf16)
    w_dt_b = jnp.pad(w_in[:, :, D_PROJ:], ((0, 0), (0, 0), (0, LANES - H_S))).astype(bf16)
    w_out_b = w_out.astype(bf16)
    w_up_b = w_up.astype(bf16)
    w_down_b = w_down.astype(bf16)
    g1 = norm1_g[:, None, :]
    g2 = norm2_g[:, None, :]
    qg = jnp.tile(q_norm_g, (1, LANES // HD))[:, None, :]
    kg = jnp.tile(k_norm_g, (1, LANES // HD))[:, None, :]
    cb_ssd = ssd_conv_b[:, None, :]
    lane_pad = ((0, 0), (0, LANES - H_S))
    dtb = jnp.pad(ssd_dt_bias, lane_pad)[:, None, :]
    a_neg = jnp.pad(-jnp.exp(ssd_a_log.astype(f32)), lane_pad)[:, None, :]
    dskip = jnp.repeat(ssd_d, HD, axis=1)[:, None, :]
    ng = ssd_norm_g[:, None, :]
    cb_ffn = ffn_conv_b[:, None, :]

    idx = np.arange(LANES)
    e2 = jnp.asarray((idx[:, None] // HD == idx[None, :] // HD).astype(np.float32), dtype=bf16)
    tri = jnp.asarray((idx[None, :] <= idx[:, None]).astype(np.float32), dtype=bf16)
    expand = jnp.asarray((idx[:, None] == np.arange(D_SSD)[None, :] // HD).astype(np.float32), dtype=bf16)
    wmask = jnp.asarray(_sample_key_weights())
    bias_band, bias_first = (jnp.asarray(a) for a in _attn_biases())
    cos_p, sin_p = _rope_tables(jnp.arange(SEQ), LANES)
    cos_s, sin_s = _rope_tables(PAST_LEN + jnp.arange(S_STEPS), GRP)

    xp = x_prompt.reshape(BATCH * SEQ, D_MODEL)
    xs = jnp.pad(x_sample, ((0, 0), (0, S_STEPS - DEC_SEQ), (0, 0))).reshape(S_ROWS, D_MODEL)

    ck = cache_win_k.reshape(DEPTH, DEC_BATCH, MAX_WINDOW, D_ATT)
    cv = cache_win_v.reshape(DEPTH, DEC_BATCH, MAX_WINDOW, D_ATT)
    ssd_ci_p = jnp.zeros((BATCH, SUBLANES, D_XBC), f32)
    ssd_ci_s = jnp.pad(state_ssd_conv, ((0, 0), (0, 0), (SUBLANES - (SSD_CONV - 1), 0), (0, 0)))
    h0_s = state_ssd.reshape(DEPTH, DEC_BATCH, D_SSD, SSD_STATE)
    h0_p = jnp.zeros((1, 1, D_SSD, SSD_STATE), f32)
    ffn_ci_p = jnp.zeros((BATCH, FFN_PAD, 2 * D_FF), f32)
    ffn_ci_s = jnp.pad(state_ffn_conv, ((0, 0), (0, 0), (SUBLANES - (FFN_CONV - 1), S_STEPS), (0, 0))
                       ).reshape(DEPTH, 1, DEC_BATCH * FFN_PAD, 2 * D_FF)

    tm_in, tm_p = 1024, 512
    k_all = v_all = None
    outs = {k: [] for k in ("ks", "vs", "cp", "cs", "hp", "hs", "fp", "fs")}
    for i in range(DEPTH):
        proj, dt_raw = _in_proj(xp, g1, w_in_b, w_dt_b, i, tm_in)
        proj3 = proj.reshape(BATCH, SEQ, D_PROJ)
        att, k_all, v_all = _attention_prompt(proj3, cos_p, sin_p, qg, kg, e2, bias_band, bias_first,
                                              i, k_all, v_all)
        ssd, h_last = _ssd(proj, dt_raw, ssd_ci_p, h0_p, ssd_conv_w, cb_ssd, dtb, a_neg, dskip, ng, tri,
                           expand, i, sample=False)
        x1 = _out_proj(att.reshape(BATCH * SEQ, D_ATT), ssd, w_out_b, xp, i, tm_p)
        xp, hs_g, hs_u = _ffn(x1, g2, w_up_b, ffn_conv_w, cb_ffn, w_down_b, ffn_ci_p, i, tm=tm_p,
                              tiles_per_seq=SEQ // tm_p, per_tile=False)
        outs["cp"].append(proj3[:, SEQ - (SSD_CONV - 1):, COL_XBC:COL_XBC + D_XBC])
        outs["hp"].append(h_last.reshape(BATCH, H_S, HD, SSD_STATE))
        last = slice(SEQ // tm_p - 1, None, SEQ // tm_p)
        hs = jnp.concatenate([hs_g[last], hs_u[last]], axis=-1)
        outs["fp"].append(hs[:, FFN_PAD - (FFN_CONV - 1):, :])

        proj_s, dt_s = _in_proj(xs, g1, w_in_b, w_dt_b, i, S_ROWS)
        att_s, kn_s = _attention_sample(proj_s, ck, cv, cos_s, sin_s, qg, kg, e2, wmask, i)
        ssd_s, h_last_s = _ssd(proj_s, dt_s, ssd_ci_s[i], h0_s, ssd_conv_w, cb_ssd, dtb, a_neg, dskip, ng,
                               tri, expand, i, sample=True)
        x1_s = _out_proj(att_s, ssd_s, w_out_b, xs, i, S_ROWS)
        xs, hs_g, hs_u = _ffn(x1_s, g2, w_up_b, ffn_conv_w, cb_ffn, w_down_b, ffn_ci_s[i], i,
                              tm=S_ROWS, tiles_per_seq=1, per_tile=True)
        p3 = proj_s.reshape(DEC_BATCH, S_STEPS, D_PROJ)
        outs["ks"].append(kn_s.reshape(DEC_BATCH, S_STEPS, H_A, HD)[:, :DEC_SEQ])
        outs["vs"].append(p3[:, :DEC_SEQ, 2 * D_ATT:3 * D_ATT].reshape(DEC_BATCH, DEC_SEQ, H_A, HD))
        outs["cs"].append(p3[:, DEC_SEQ - (SSD_CONV - 1):DEC_SEQ, COL_XBC:COL_XBC + D_XBC])
        outs["hs"].append(h_last_s.reshape(DEC_BATCH, H_S, HD, SSD_STATE))
        hs = jnp.concatenate([hs_g[0], hs_u[0]], axis=-1).reshape(DEC_BATCH, FFN_PAD, 2 * D_FF)
        lo = SUBLANES + DEC_SEQ - (FFN_CONV - 1)
        outs["fs"].append(hs[:, lo:lo + FFN_CONV - 1, :])

    yp = xp.reshape(BATCH, SEQ, D_MODEL)
    ys = xs.reshape(DEC_BATCH, S_STEPS, D_MODEL)[:, :DEC_SEQ]
    st = lambda k: jnp.stack(outs[k])
    win_shape = (DEPTH, BATCH, SEQ, H_A, HD)
    return (yp, ys, k_all.reshape(win_shape), v_all.reshape(win_shape), st("ks"), st("vs"),
            st("cp"), st("cs"), st("hp"), st("hs"), st("fp"), st("fs"))
```

```python
import functools

import numpy as np
import jax
import jax.numpy as jnp
from jax import lax
from jax.experimental import pallas as pl
from jax.experimental.pallas import tpu as pltpu

f32 = jnp.float32
bf16 = jnp.bfloat16

D_MODEL = 2048
BATCH = 4
SEQ = 2048
DEPTH = 4
DEC_BATCH = 8
DEC_SEQ = 4
PAST_LEN = 16384
HD = 64
D_ATT = 1024
H_A = 16
DILATIONS = (1, 4, 16)
WIN_KEYS = 128
MAX_WINDOW = 2048
ROPE_THETA = 10000.0
D_SSD = 1024
H_S = 16
SSD_STATE = 128
SSD_GROUPS = 4
SSD_CONV = 4
D_XBC = 2048
D_FF = 5632
FFN_CONV = 3
EPS = 1e-6

LANES = 128
SUBLANES = 8
CHUNK = 128
S_STEPS = SUBLANES
S_ROWS = S_STEPS * DEC_BATCH

COL_Z = 3 * D_ATT
COL_XBC = COL_Z + D_SSD
D_PROJ = COL_XBC + D_XBC

VMEM_LIMIT = 56 * 1024 * 1024


def _cparams(sem):
    return pltpu.CompilerParams(dimension_semantics=sem, vmem_limit_bytes=VMEM_LIMIT)


def _dot(a, b):
    return jnp.dot(a, b, preferred_element_type=f32)


def _dot_nt(a, b):
    return lax.dot_general(a, b, (((1,), (1,)), ((), ())), preferred_element_type=f32)


def _split(x, parts):
    out = []
    rem = x
    for p in range(parts):
        hi = rem.astype(bf16)
        out.append(hi)
        if p + 1 < parts:
            rem = rem - hi.astype(f32)
    return out


def _split_dot(x, w, parts):
    acc = None
    for hi in _split(x, parts):
        t = _dot(hi, w)
        acc = t if acc is None else acc + t
    return acc


def _silu(x):
    return x * (1.0 / (1.0 + jnp.exp(-x)))


def _softplus(x):
    return jnp.maximum(x, 0.0) + jnp.log1p(jnp.exp(-jnp.abs(x)))


def _rmsnorm_to(x_ref, g_ref, xn_ref, rows):
    chunk = min(rows, 256)

    def body(i, c):
        r = pl.multiple_of(i * chunk, chunk)
        x = x_ref[pl.ds(r, chunk), :]
        ms = jnp.mean(x * x, axis=-1, keepdims=True)
        xn_ref[pl.ds(r, chunk), :] = (x * lax.rsqrt(ms + EPS) * g_ref[...]).astype(bf16)
        return c

    lax.fori_loop(0, rows // chunk, body, 0)


def _inproj_kernel(x_ref, g_ref, w_ref, wdt_ref, o_ref, dt_ref, xn_ref, *, tm):
    @pl.when(pl.program_id(1) == 0)
    def _():
        _rmsnorm_to(x_ref, g_ref, xn_ref, tm)
        dt_ref[...] = _dot(xn_ref[...], wdt_ref[...])

    o_ref[...] = _dot(xn_ref[...], w_ref[...])


def _in_proj(x, g, w, wdt, layer, tm, tn=1536):
    rows = x.shape[0]
    return pl.pallas_call(
        functools.partial(_inproj_kernel, tm=tm),
        grid=(rows // tm, D_PROJ // tn),
        in_specs=[
            pl.BlockSpec((tm, D_MODEL), lambda i, j: (i, 0)),
            pl.BlockSpec((None, 1, D_MODEL), lambda i, j: (layer, 0, 0)),
            pl.BlockSpec((None, D_MODEL, tn), lambda i, j: (layer, 0, j)),
            pl.BlockSpec((None, D_MODEL, LANES), lambda i, j: (layer, 0, 0)),
        ],
        out_specs=[pl.BlockSpec((tm, tn), lambda i, j: (i, j)),
                   pl.BlockSpec((tm, LANES), lambda i, j: (i, 0))],
        out_shape=[jax.ShapeDtypeStruct((rows, D_PROJ), f32),
                   jax.ShapeDtypeStruct((rows, LANES), f32)],
        scratch_shapes=[pltpu.VMEM((tm, D_MODEL), bf16)],
        compiler_params=_cparams(("arbitrary", "arbitrary")),
        name="in_proj",
    )(x, g, w, wdt)


def _outproj_kernel(a_ref, s_ref, wa_ref, ws_ref, x_ref, o_ref):
    acc = _dot(a_ref[...].astype(bf16), wa_ref[...])
    acc = acc + _dot(s_ref[...].astype(bf16), ws_ref[...])
    o_ref[...] = x_ref[...] + acc


def _out_proj(att, ssd, w, x, layer, tm, tn=D_MODEL):
    rows = x.shape[0]
    return pl.pallas_call(
        _outproj_kernel,
        grid=(rows // tm, D_MODEL // tn),
        in_specs=[
            pl.BlockSpec((tm, D_ATT), lambda i, j: (i, 0)),
            pl.BlockSpec((tm, D_SSD), lambda i, j: (i, 0)),
            pl.BlockSpec((None, D_ATT, tn), lambda i, j: (layer, 0, j)),
            pl.BlockSpec((None, D_SSD, tn), lambda i, j: (layer, 1, j)),
            pl.BlockSpec((tm, tn), lambda i, j: (i, j)),
        ],
        out_specs=pl.BlockSpec((tm, tn), lambda i, j: (i, j)),
        out_shape=jax.ShapeDtypeStruct((rows, D_MODEL), f32),
        compiler_params=_cparams(("arbitrary", "arbitrary")),
        name="out_proj",
    )(att, ssd, w, w, x)


FFN_PAD = 2 * SUBLANES


def _ffn_kernel(x_ref, g_ref, wg_ref, wu_ref, cwg_ref, cwu_ref, cbg_ref, cbu_ref, wd_ref,
                cig_ref, ciu_ref, o_ref, hsg_ref, hsu_ref,
                xn_ref, acc_ref, hbg0_ref, hbu0_ref, hbg1_ref, hbu1_ref, cag_ref, cau_ref,
                act0_ref, act1_ref, *, tm, tiles_per_seq, per_tile, rc, n_chunks):
    i = pl.program_id(0)
    j = pl.program_id(1)
    nj = n_chunks
    pad = FFN_PAD
    hbs = ((hbg0_ref, hbu0_ref), (hbg1_ref, hbu1_ref))
    acts = (act0_ref, act1_ref)
    gate = (wg_ref, cig_ref, cag_ref, hsg_ref, cwg_ref, cbg_ref)
    up = (wu_ref, ciu_ref, cau_ref, hsu_ref, cwu_ref, cbu_ref)

    def start_tile():
        _rmsnorm_to(x_ref, g_ref, xn_ref, tm)
        acc_ref[...] = jnp.zeros_like(acc_ref)

    def conv_act(hb, act_ref, r, out_rows):
        convs = []
        for hb_ref, (_, _, _, _, cw_ref, cb_ref) in zip(hb, (gate, up)):
            y = cb_ref[...] + hb_ref[r - 2:r - 2 + rc, :] * cw_ref[0:1, :]
            y = y + hb_ref[r - 1:r - 1 + rc, :] * cw_ref[1:2, :]
            y = y + hb_ref[r:r + rc, :] * cw_ref[2:3, :]
            convs.append(y)
        act_ref[out_rows, :] = (_silu(convs[0]) * convs[1]).astype(bf16)

    def down_proj(act_ref):
        acc_ref[...] += _dot(act_ref[...], wd_ref[...])

    if per_tile:
        @pl.when(j == 0)
        def _():
            start_tile()

        for hb_ref, (w_ref, ci_ref, _, hs_ref, _, _) in zip(hbs[0], (gate, up)):
            hb_ref[...] = ci_ref[...]
            h = _dot(xn_ref[...], w_ref[...])
            for b in range(tm // SUBLANES):
                hb_ref[b * pad + SUBLANES:(b + 1) * pad, :] = h[b * SUBLANES:(b + 1) * SUBLANES, :]
            hs_ref[...] = hb_ref[...]
        for b in range(tm // SUBLANES):
            conv_act(hbs[0], act0_ref, b * pad + SUBLANES, slice(b * SUBLANES, (b + 1) * SUBLANES))
        down_proj(act0_ref)

        @pl.when(j == nj - 1)
        def _():
            o_ref[...] = x_ref[...] + acc_ref[...]
        return

    first = (i % tiles_per_seq) == 0
    tf = act0_ref.shape[1]

    def step(hb_up, conv, act_down):
        pieces = []
        if hb_up is not None:
            for hb_ref, (w_ref, ci_ref, ca_ref, _, _, _) in zip(hb_up, (gate, up)):
                hb_ref[0:pad, :] = jnp.where(first, ci_ref[...], ca_ref[j])
                for n in range(2):
                    cols = slice(n * tf // 2, (n + 1) * tf // 2)

                    def piece(hb_ref=hb_ref, w_ref=w_ref, cols=cols):
                        hb_ref[pad:pad + tm, cols] = _dot(xn_ref[...], w_ref[:, cols])

                    pieces.append(piece)
        if act_down is not None:
            for n in range(4):
                cols = slice(n * D_MODEL // 4, (n + 1) * D_MODEL // 4)

                def piece(cols=cols):
                    acc_ref[:, cols] += _dot(act_down[...], wd_ref[:, cols])

                pieces.insert(min(2 * n + 1, len(pieces)), piece)
        chunks = []
        if conv is not None:
            for c in range(tm // rc):
                chunks.append(functools.partial(conv_act, conv[0], conv[1], pad + c * rc,
                                                slice(c * rc, (c + 1) * rc)))
        for k in range(max(len(pieces), len(chunks))):
            if k < len(pieces):
                pieces[k]()
            if k < len(chunks):
                chunks[k]()
        if hb_up is not None:
            for hb_ref, (_, _, ca_ref, hs_ref, _, _) in zip(hb_up, (gate, up)):
                ca_ref[j] = hb_ref[tm:tm + pad, :]
                hs_ref[...] = hb_ref[tm:tm + pad, :]

    @pl.when(j == 0)
    def _():
        start_tile()
        step(hbs[0], None, None)

    @pl.when(j == 1)
    def _():
        step(hbs[1], (hbs[0], acts[0]), None)

    for p in (0, 1):
        @pl.when((j > 1) & (j < nj) & (j % 2 == p))
        def _():
            step(hbs[p], (hbs[1 - p], acts[1 - p]), acts[p])

    p = nj % 2
    @pl.when(j == nj)
    def _():
        step(None, (hbs[1 - p], acts[1 - p]), acts[p])

    @pl.when(j == nj + 1)
    def _():
        step(None, None, acts[1 - p])
        o_ref[...] = x_ref[...] + acc_ref[...]


def _ffn(x, g, w_up, cw, cb, w_down, cin, layer, *, tm, tiles_per_seq, per_tile, tf=512):
    rows = x.shape[0]
    nj = D_FF // tf
    pad = FFN_PAD
    rc = SUBLANES if per_tile else min(tm, 64)
    hb_rows = (tm // SUBLANES) * pad if per_tile else tm + pad
    st_rows = hb_rows if per_tile else pad
    kern = functools.partial(_ffn_kernel, tm=tm, tiles_per_seq=tiles_per_seq, per_tile=per_tile, rc=rc,
                             n_chunks=nj)
    if per_tile:
        ju = jc = jd = lambda j: j
        steps = nj
    else:
        ju = lambda j: jnp.minimum(j, nj - 1)
        jc = lambda j: jnp.clip(j - 1, 0, nj - 1)
        jd = lambda j: jnp.clip(j - 2, 0, nj - 1)
        steps = nj + 2
    return pl.pallas_call(
        kern,
        grid=(rows // tm, steps),
        in_specs=[
            pl.BlockSpec((tm, D_MODEL), lambda i, j: (i, 0)),
            pl.BlockSpec((None, 1, D_MODEL), lambda i, j: (layer, 0, 0)),
            pl.BlockSpec((None, D_MODEL, tf), lambda i, j: (layer, 0, ju(j))),
            pl.BlockSpec((None, D_MODEL, tf), lambda i, j: (layer, 0, nj + ju(j))),
            pl.BlockSpec((None, FFN_CONV, tf), lambda i, j: (layer, 0, jc(j))),
            pl.BlockSpec((None, FFN_CONV, tf), lambda i, j: (layer, 0, nj + jc(j))),
            pl.BlockSpec((None, 1, tf), lambda i, j: (layer, 0, jc(j))),
            pl.BlockSpec((None, 1, tf), lambda i, j: (layer, 0, nj + jc(j))),
            pl.BlockSpec((None, tf, D_MODEL), lambda i, j: (layer, jd(j), 0)),
            pl.BlockSpec((None, st_rows, tf), lambda i, j: (i // tiles_per_seq, 0, ju(j))),
            pl.BlockSpec((None, st_rows, tf), lambda i, j: (i // tiles_per_seq, 0, nj + ju(j))),
        ],
        out_specs=[
            pl.BlockSpec((tm, D_MODEL), lambda i, j: (i, 0)),
            pl.BlockSpec((None, st_rows, tf), lambda i, j: (i, 0, ju(j))),
            pl.BlockSpec((None, st_rows, tf), lambda i, j: (i, 0, ju(j))),
        ],
        out_shape=[
            jax.ShapeDtypeStruct((rows, D_MODEL), f32),
            jax.ShapeDtypeStruct((rows // tm, st_rows, D_FF), f32),
            jax.ShapeDtypeStruct((rows // tm, st_rows, D_FF), f32),
        ],
        scratch_shapes=[
            pltpu.VMEM((tm, D_MODEL), bf16),
            pltpu.VMEM((tm, D_MODEL), f32),
            pltpu.VMEM((hb_rows, tf), f32),
            pltpu.VMEM((hb_rows, tf), f32),
            pltpu.VMEM((hb_rows, tf), f32),
            pltpu.VMEM((hb_rows, tf), f32),
            pltpu.VMEM((nj, pad, tf), f32),
            pltpu.VMEM((nj, pad, tf), f32),
            pltpu.VMEM((tm, tf), bf16),
            pltpu.VMEM((tm, tf), bf16),
        ],
        compiler_params=_cparams(("arbitrary", "arbitrary")),
        name="conv_ffn",
    )(x, g, w_up, w_up, cw, cw, cb, cb, w_down, cin, cin)


def _head_norm_rope(x, gain, cos, sin, e2, low_half):
    ssq = _split_dot(x * x, e2, 2)
    xn = x * lax.rsqrt(ssq * (1.0 / HD) + EPS) * gain
    rot = jnp.where(low_half, pltpu.roll(xn, LANES - HD // 2, 1), pltpu.roll(xn, HD // 2, 1))
    return xn * cos + rot * sin


def _attn_block(q_ref, k_ref, ve_ref, q_rows, k_rows, bias_ref, lane_lo):
    qb = q_ref[q_rows, :]
    zero = jnp.zeros_like(qb)
    q2 = jnp.concatenate([jnp.where(lane_lo, qb, zero), jnp.where(lane_lo, zero, qb)], axis=0)
    s = _dot_nt(q2, k_ref[k_rows, :]) + bias_ref[...]
    m = jnp.max(s, axis=1, keepdims=True)
    p = jnp.exp(s - m).astype(bf16)
    r = _dot(p, ve_ref[k_rows, :])
    shape = (CHUNK, LANES)
    o = jnp.where(lane_lo, r[:CHUNK, :LANES], r[CHUNK:, :LANES])
    le = jnp.where(lane_lo, r[:CHUNK, LANES:], r[CHUNK:, LANES:])
    me = jnp.where(lane_lo, jnp.broadcast_to(m[:CHUNK], shape), jnp.broadcast_to(m[CHUNK:], shape))
    return o, me, le


def _attn_kernel(*refs, aliased):
    (q_ref, k_ref, v_ref, cos_ref, sin_ref, qg_ref, kg_ref, e2_ref, bband_ref, bfirst_ref) = refs[:10]
    refs = refs[12:] if aliased else refs[10:]
    (att_ref, ko_ref, vo_ref, qf_ref, q4f_ref, k4f_ref, v4f_ref,
     q1_ref, k1_ref, v1_ref, q4_ref, k4_ref, v4_ref, q16_ref, k16_ref, v16_ref,
     o1_ref, m1_ref, l1_ref, o4_ref, m4_ref, l4_ref, o16_ref, m16_ref, l16_ref) = refs
    lane_lo = lax.broadcasted_iota(jnp.int32, (CHUNK, LANES), 1) < HD
    rows_a = 256
    lane_a = lax.broadcasted_iota(jnp.int32, (rows_a, LANES), 1)
    low_half_a = (lane_a % HD) < (HD // 2)

    @pl.when((pl.program_id(0) == 0) & (pl.program_id(1) == 0))
    def _():
        ones = jnp.ones((SEQ, LANES), bf16)
        v1_ref[:, LANES:] = ones
        v4_ref[:, LANES:] = ones
        v16_ref[:, LANES:] = ones

    def prep(i, c):
        r = pl.multiple_of(i * rows_a, rows_a)
        sl = pl.ds(r, rows_a)
        cos = cos_ref[sl, :]
        sin = sin_ref[sl, :]
        q = _head_norm_rope(q_ref[sl, :], qg_ref[...], cos, sin, e2_ref[...], low_half_a) * (HD ** -0.5)
        k = _head_norm_rope(k_ref[sl, :], kg_ref[...], cos, sin, e2_ref[...], low_half_a)
        v = v_ref[sl, :]
        qf_ref[sl, :] = q
        ko_ref[sl, :] = k
        vo_ref[sl, :] = v
        q1_ref[sl, :] = q.astype(bf16)
        k1_ref[sl, :] = k.astype(bf16)
        v1_ref[sl, 0:LANES] = v.astype(bf16)
        return c

    lax.fori_loop(0, SEQ // rows_a, prep, 0, unroll=2)

    quarter = SEQ // 4
    streams = ((qf_ref, q4f_ref, q4_ref, q16_ref), (ko_ref, k4f_ref, k4_ref, k16_ref),
               (v_ref, v4f_ref, v4_ref, v16_ref))
    for r in range(4):
        src = pl.ds(r, quarter, stride=4)
        dst = pl.ds(r * quarter, quarter)
        for tok_ref, d4f_ref, d4_ref, _ in streams:
            x = tok_ref[src, :]
            d4f_ref[dst, :] = x
            d4_ref[dst, 0:LANES] = x.astype(bf16)
    for r16 in range(16):
        src = pl.ds((r16 % 4) * quarter + r16 // 4, CHUNK, stride=4)
        dst = pl.ds(r16 * CHUNK, CHUNK)
        for _, d4f_ref, _, d16_ref in streams:
            d16_ref[dst, 0:LANES] = d4f_ref[src, :].astype(bf16)

    def first_block(qd, kd, vd, outs, base, dst):
        rows = pl.ds(base, CHUNK)
        res = _attn_block(qd, kd, vd, rows, rows, bfirst_ref, lane_lo)
        for ref, val in zip(outs, res):
            ref[dst, :] = val

    def band_block(qd, kd, vd, outs, q0, dst):
        k_rows = pl.ds(q0 - CHUNK, 2 * CHUNK)
        res = _attn_block(qd, kd, vd, pl.ds(q0, CHUNK), k_rows, bband_ref, lane_lo)
        for ref, val in zip(outs, res):
            ref[dst, :] = val

    p1 = (q1_ref, k1_ref, v1_ref, (o1_ref, m1_ref, l1_ref))
    p4 = (q4_ref, k4_ref, v4_ref, (o4_ref, m4_ref, l4_ref))
    p16 = (q16_ref, k16_ref, v16_ref, (o16_ref, m16_ref, l16_ref))

    first_block(*p1, 0, pl.ds(0, CHUNK))
    for r in range(4):
        first_block(*p4, r * quarter, pl.ds(r * quarter, CHUNK))

    for c in range(1, SEQ // CHUNK):
        band_block(*p1, c * CHUNK, pl.ds(c * CHUNK, CHUNK))
    for r in range(4):
        for c in range(1, quarter // CHUNK):
            q0 = r * quarter + c * CHUNK
            band_block(*p4, q0, pl.ds(q0, CHUNK))
    for r16 in range(16):
        dst = pl.ds((r16 % 4) * quarter + r16 // 4, CHUNK, stride=4)
        first_block(*p16, r16 * CHUNK, dst)

    for r in range(4):
        def fin(mb, c, r=r):
            sl = pl.ds(pl.multiple_of(r * quarter + mb * CHUNK, CHUNK), CHUNK)
            tok = pl.ds(r + 4 * CHUNK * mb, CHUNK, stride=4)
            m1, m4, m16 = m1_ref[tok, :], m4_ref[sl, :], m16_ref[sl, :]
            mx = jnp.maximum(jnp.maximum(m1, m4), m16)
            a1, a4, a16 = jnp.exp(m1 - mx), jnp.exp(m4 - mx), jnp.exp(m16 - mx)
            num = o1_ref[tok, :] * a1 + o4_ref[sl, :] * a4 + o16_ref[sl, :] * a16
            den = l1_ref[tok, :] * a1 + l4_ref[sl, :] * a4 + l16_ref[sl, :] * a16
            att_ref[tok, :] = num / den
            return c

        lax.fori_loop(0, quarter // CHUNK, fin, 0, unroll=2)


def _attention_prompt(proj3, cos, sin, qg, kg, e2, bias_band, bias_first, layer, k_all, v_all):
    nhp = D_ATT // LANES
    blk = lambda off: pl.BlockSpec((None, SEQ, LANES), lambda b, h: (b, 0, off + h))
    tab = pl.BlockSpec((SEQ, LANES), lambda b, h: (0, 0))
    gain = pl.BlockSpec((None, 1, LANES), lambda b, h: (layer, 0, 0))
    const = lambda shape: pl.BlockSpec(shape, lambda b, h: (0, 0))
    stacked = pl.BlockSpec((None, None, SEQ, LANES), lambda b, h: (layer, b, 0, h))
    bscr = lambda w: pltpu.VMEM((SEQ, w), bf16)
    fscr = lambda: pltpu.VMEM((SEQ, LANES), f32)
    aliased = k_all is not None
    in_specs = [blk(0), blk(nhp), blk(2 * nhp), tab, tab, gain, gain, const((LANES, LANES)),
                const((2 * CHUNK, 2 * CHUNK)), const((2 * CHUNK, CHUNK))]
    args = [proj3, proj3, proj3, cos, sin, qg, kg, e2, bias_band, bias_first]
    aliases = {}
    if aliased:
        in_specs += [pl.BlockSpec(memory_space=pl.ANY)] * 2
        aliases = {len(args): 1, len(args) + 1: 2}
        args += [k_all, v_all]
    return pl.pallas_call(
        functools.partial(_attn_kernel, aliased=aliased),
        grid=(BATCH, nhp),
        in_specs=in_specs,
        out_specs=[pl.BlockSpec((None, SEQ, LANES), lambda b, h: (b, 0, h)), stacked, stacked],
        out_shape=[jax.ShapeDtypeStruct((BATCH, SEQ, D_ATT), f32),
                   jax.ShapeDtypeStruct((DEPTH, BATCH, SEQ, D_ATT), f32),
                   jax.ShapeDtypeStruct((DEPTH, BATCH, SEQ, D_ATT), f32)],
        scratch_shapes=([fscr() for _ in range(4)] + [bscr(LANES), bscr(LANES), bscr(2 * LANES)] * 3
                        + [fscr() for _ in range(9)]),
        input_output_aliases=aliases,
        compiler_params=_cparams(("arbitrary", "arbitrary")),
        name="attn_prompt",
    )(*args)


def _attn_biases():
    qi = (np.arange(2 * CHUNK) % CHUNK)[:, None]
    ki = np.arange(2 * CHUNK)[None, :]
    band = np.where((ki >= qi) & (ki <= qi + CHUNK), 0.0, -np.inf).astype(np.float32)
    first = np.where(ki[:, :CHUNK] <= qi, 0.0, -np.inf).astype(np.float32)
    return band, first


GRP = 4 * HD
TAIL = WIN_KEYS * DILATIONS[1]
N_STRIDED = (MAX_WINDOW // DILATIONS[2]) * DEC_SEQ
N_CACHED = N_STRIDED + TAIL
KPAD = N_CACHED + LANES


def _attn_sample_kernel(q_ref, k_ref, v_ref, cks_ref, ckt_ref, cvs_ref, cvt_ref, cos_ref, sin_ref,
                        qg_ref, kg_ref, e2_ref, w_ref, att_ref, kn_ref, kc_ref, vc_ref):
    lane = lax.broadcasted_iota(jnp.int32, (S_STEPS, LANES), 1)
    low_half = (lane % HD) < (HD // 2)
    e2 = e2_ref[...]
    qs, ks = [], []
    for t in range(GRP // LANES):
        cs = slice(t * LANES, (t + 1) * LANES)
        cos = cos_ref[:, cs]
        sin = sin_ref[:, cs]
        qs.append(_head_norm_rope(q_ref[:, cs], qg_ref[...], cos, sin, e2, low_half) * (HD ** -0.5))
        ks.append(_head_norm_rope(k_ref[:, cs], kg_ref[...], cos, sin, e2, low_half))
    q = jnp.concatenate(qs, axis=1)
    k = jnp.concatenate(ks, axis=1)
    kn_ref[...] = k

    kc_ref[0:N_STRIDED, :] = cks_ref[...].astype(bf16)
    vc_ref[0:N_STRIDED, :] = cvs_ref[...].astype(bf16)
    kc_ref[N_STRIDED:N_CACHED, :] = ckt_ref[...].astype(bf16)
    vc_ref[N_STRIDED:N_CACHED, :] = cvt_ref[...].astype(bf16)
    kc_ref[N_CACHED:KPAD, :] = jnp.zeros((KPAD - N_CACHED, GRP), bf16)
    vc_ref[N_CACHED:KPAD, :] = jnp.zeros((KPAD - N_CACHED, GRP), bf16)
    kc_ref[N_CACHED:N_CACHED + 2 * S_STEPS, :] = jnp.concatenate(
        [k, jnp.zeros_like(k)], axis=0).astype(bf16)
    vc_ref[N_CACHED:N_CACHED + 2 * S_STEPS, :] = jnp.concatenate(
        [v_ref[...], jnp.zeros_like(k)], axis=0).astype(bf16)

    nq = (GRP // HD) * S_STEPS
    qt = jnp.concatenate([q] * (GRP // HD) + [jnp.zeros((LANES - nq, GRP), f32)], axis=0)
    row_h = lax.broadcasted_iota(jnp.int32, (LANES, GRP), 0) // S_STEPS
    lane_h = lax.broadcasted_iota(jnp.int32, (LANES, GRP), 1) // HD
    same_head = row_h == lane_h
    qt = jnp.where(same_head, qt, 0.0).astype(bf16)
    s = _dot_nt(kc_ref[...], qt)
    w = w_ref[...]
    keep = w > 0.0
    m = jnp.max(jnp.where(keep, s, -jnp.inf), axis=0, keepdims=True)
    e = jnp.where(keep, w * jnp.exp(s - m), 0.0)
    den = jnp.sum(e, axis=0, keepdims=True)
    pt = jnp.transpose(e / den).astype(bf16)
    res = _dot(pt, vc_ref[...])
    res = jnp.where(same_head, res, 0.0)
    out = res[0:S_STEPS]
    for h in range(1, GRP // HD):
        out = out + res[h * S_STEPS:(h + 1) * S_STEPS]
    att_ref[...] = out


def _attention_sample(proj, cks, ckt, cvs, cvt, cos, sin, qg, kg, e2, wmask, layer):
    ng = D_ATT // GRP
    col = lambda off: pl.BlockSpec((S_STEPS, GRP), lambda b, g: (b, off + g))
    cache = lambda n: pl.BlockSpec((None, None, n, GRP), lambda b, g: (layer, b, 0, g))
    tab = pl.BlockSpec((S_STEPS, GRP), lambda b, g: (0, 0))
    gain = pl.BlockSpec((None, 1, LANES), lambda b, g: (layer, 0, 0))
    return pl.pallas_call(
        _attn_sample_kernel,
        grid=(DEC_BATCH, ng),
        in_specs=[col(0), col(ng), col(2 * ng), cache(N_STRIDED), cache(TAIL), cache(N_STRIDED),
                  cache(TAIL), tab, tab, gain, gain,
                  pl.BlockSpec((LANES, LANES), lambda b, g: (0, 0)),
                  pl.BlockSpec((KPAD, LANES), lambda b, g: (0, 0))],
        out_specs=[pl.BlockSpec((S_STEPS, GRP), lambda b, g: (b, g))] * 2,
        out_shape=[jax.ShapeDtypeStruct((S_ROWS, D_ATT), f32)] * 2,
        scratch_shapes=[pltpu.VMEM((KPAD, GRP), bf16), pltpu.VMEM((KPAD, GRP), bf16)],
        compiler_params=_cparams(("arbitrary", "arbitrary")),
        name="attn_sample",
    )(proj, proj, proj, cks, ckt, cvs, cvt, cos, sin, qg, kg, e2, wmask)


def _sample_key_weights():
    pos = np.full((KPAD,), -1, np.int64)
    s = np.arange(N_STRIDED)
    strided_rows = (s // DEC_SEQ) * DILATIONS[2] + s % DEC_SEQ
    pos[:N_STRIDED] = np.where(strided_rows < MAX_WINDOW - TAIL, strided_rows, -1)
    pos[N_STRIDED:N_CACHED] = MAX_WINDOW - TAIL + np.arange(TAIL)
    pos[N_CACHED:N_CACHED + DEC_SEQ] = MAX_WINDOW + np.arange(DEC_SEQ)
    w = np.zeros((KPAD, LANES), np.float32)
    for t in range(S_STEPS):
        mult = np.zeros((KPAD,), np.float32)
        if t < DEC_SEQ:
            dist = np.where(pos >= 0, MAX_WINDOW + t - pos, -1)
            for d in DILATIONS:
                mult += ((dist >= 0) & (dist % d == 0) & (dist <= WIN_KEYS * d)).astype(np.float32)
        else:
            mult[N_CACHED + t] = 1.0
        for h in range(GRP // HD):
            w[:, h * S_STEPS + t] = mult
    w[0, (GRP // HD) * S_STEPS:] = 1.0
    return w


def _ssd_kernel(z_ref, xbc_ref, dt_ref, ci_ref, h0_ref, cw_ref, cb_ref, dtb_ref, a_ref, dsk_ref,
                ng_ref, tri_ref, exp_ref, y_ref, ho_ref,
                cbuf_ref, xc_ref, st_ref, zb_ref, db_ref, yb_ref,
                *, padded, valid, has_h0):
    c = pl.program_id(1)
    nc = pl.num_programs(1)

    if padded:
        zb_ref[...] = jnp.zeros_like(zb_ref)
        db_ref[...] = jnp.zeros_like(db_ref)
        cbuf_ref[SUBLANES:, :] = jnp.zeros((CHUNK, D_XBC), f32)
        zb_ref[0:S_STEPS, :] = z_ref[...]
        db_ref[0:S_STEPS, :] = dt_ref[...]
        cbuf_ref[SUBLANES:SUBLANES + S_STEPS, :] = xbc_ref[...]
        zsrc, dsrc = zb_ref, db_ref
    else:
        cbuf_ref[SUBLANES:, :] = xbc_ref[...]
        zsrc, dsrc = z_ref, dt_ref

    @pl.when(c == 0)
    def _():
        cbuf_ref[0:SUBLANES, :] = ci_ref[...]
        if has_h0:
            st_ref[...] = jnp.transpose(h0_ref[...])
        else:
            st_ref[...] = jnp.zeros_like(st_ref)

    for t in range(D_XBC // 256):
        cs = slice(t * 256, (t + 1) * 256)
        acc = cb_ref[:, cs] + cbuf_ref[5:5 + CHUNK, cs] * cw_ref[0:1, cs]
        acc = acc + cbuf_ref[6:6 + CHUNK, cs] * cw_ref[1:2, cs]
        acc = acc + cbuf_ref[7:7 + CHUNK, cs] * cw_ref[2:3, cs]
        acc = acc + cbuf_ref[8:8 + CHUNK, cs] * cw_ref[3:4, cs]
        xc_ref[:, cs] = _silu(acc)
    cbuf_ref[0:SUBLANES, :] = cbuf_ref[CHUNK:CHUNK + SUBLANES, :]

    tri = tri_ref[...]
    expand = exp_ref[...]
    dt = _softplus(dsrc[...] + dtb_ref[...])
    a = dt * a_ref[...]
    a_cum = None
    for hi in _split(a, 3):
        t_ = _dot(tri, hi)
        a_cum = t_ if a_cum is None else a_cum + t_
    a_cum_t = jnp.transpose(a_cum)
    dt_e = _split_dot(dt, expand, 2)
    acum_e = _split_dot(a_cum, expand, 3)
    alast_e = acum_e[valid - 1:valid, :]
    row = lax.broadcasted_iota(jnp.int32, (CHUNK, D_SSD), 0)

    xs = xc_ref[:, 0:D_SSD]
    xdt = xs * dt_e
    xdt_b = xdt.astype(bf16)
    xend_b = jnp.where(row < valid, xdt * jnp.exp(alast_e - acum_e), 0.0).astype(bf16)

    ii = lax.broadcasted_iota(jnp.int32, (CHUNK, CHUNK), 0)
    jj = lax.broadcasted_iota(jnp.int32, (CHUNK, CHUNK), 1)
    causal = jj <= ii
    lane_lo = lax.broadcasted_iota(jnp.int32, (CHUNK, LANES), 1) < HD
    hpg = H_S // SSD_GROUPS
    gw = hpg * HD
    for g in range(SSD_GROUPS):
        gs = slice(g * gw, (g + 1) * gw)
        b0 = D_SSD + g * SSD_STATE
        c0 = D_SSD + (SSD_GROUPS + g) * SSD_STATE
        bm = xc_ref[:, b0:b0 + SSD_STATE]
        cm = xc_ref[:, c0:c0 + SSD_STATE].astype(bf16)
        cbm = _dot_nt(cm, bm.astype(bf16))
        bt = jnp.transpose(bm).astype(bf16)
        st_new = _dot(bt, xend_b[:, gs])
        y_off = _dot(cm, st_ref[:, gs].astype(bf16))
        for pr in range(hpg // 2):
            ys = []
            ps = slice(g * gw + pr * LANES, g * gw + (pr + 1) * LANES)
            for hh in range(2):
                h = g * hpg + pr * 2 + hh
                seg = a_cum[:, h:h + 1] - a_cum_t[h:h + 1, :]
                gm = (cbm * jnp.exp(jnp.where(causal, seg, -jnp.inf))).astype(bf16)
                ys.append(_dot(gm, xdt_b[:, ps]))
            yb_ref[:, ps] = jnp.where(lane_lo, ys[0], ys[1])
        yb_ref[:, gs] = yb_ref[:, gs] + y_off * jnp.exp(acum_e[:, gs])
        st_ref[:, gs] = jnp.exp(alast_e[:, gs]) * st_ref[:, gs] + st_new

    y = yb_ref[...] + dsk_ref[...] * xs
    y = y * _silu(zsrc[...])
    ms = jnp.mean(y * y, axis=-1, keepdims=True)
    y = y * lax.rsqrt(ms + EPS) * ng_ref[...]
    if padded:
        y_ref[...] = y[0:S_STEPS]
    else:
        y_ref[...] = y.astype(y_ref.dtype)

    @pl.when(c == nc - 1)
    def _():
        ho_ref[...] = jnp.transpose(st_ref[...])


def _ssd(proj, dt_raw, cinit, h0, cw, cb, dtb, a_neg, dskip, ng, tri, expand, layer, *, sample):
    if sample:
        nb, nc, rows = DEC_BATCH, 1, S_STEPS
        y_dtype = f32
        h0_spec = pl.BlockSpec((None, None, D_SSD, SSD_STATE), lambda b, c: (layer, b, 0, 0))
    else:
        nb, nc, rows = BATCH, SEQ // CHUNK, CHUNK
        y_dtype = bf16
        h0_spec = pl.BlockSpec((None, None, D_SSD, SSD_STATE), lambda b, c: (0, 0, 0, 0))
    rowblk = lambda width, colblk: pl.BlockSpec((rows, width), lambda b, c: (b * nc + c, colblk))
    vec = lambda width: pl.BlockSpec((None, 1, width), lambda b, c: (layer, 0, 0))
    const = lambda shape: pl.BlockSpec(shape, lambda b, c: (0, 0))
    kern = functools.partial(_ssd_kernel, padded=sample, valid=DEC_SEQ if sample else CHUNK,
                             has_h0=sample)
    return pl.pallas_call(
        kern,
        grid=(nb, nc),
        in_specs=[
            rowblk(D_SSD, COL_Z // D_SSD),
            rowblk(D_XBC, COL_XBC // D_XBC),
            rowblk(LANES, 0),
            pl.BlockSpec((None, SUBLANES, D_XBC), lambda b, c: (b, 0, 0)),
            h0_spec,
            pl.BlockSpec((None, SSD_CONV, D_XBC), lambda b, c: (layer, 0, 0)),
            vec(D_XBC), vec(LANES), vec(LANES), vec(D_SSD), vec(D_SSD),
            const((CHUNK, CHUNK)), const((LANES, D_SSD)),
        ],
        out_specs=[rowblk(D_SSD, 0), pl.BlockSpec((None, D_SSD, SSD_STATE), lambda b, c: (b, 0, 0))],
        out_shape=[jax.ShapeDtypeStruct((nb * nc * rows, D_SSD), y_dtype),
                   jax.ShapeDtypeStruct((nb, D_SSD, SSD_STATE), f32)],
        scratch_shapes=[
            pltpu.VMEM((CHUNK + SUBLANES, D_XBC), f32),
            pltpu.VMEM((CHUNK, D_XBC), f32),
            pltpu.VMEM((SSD_STATE, D_SSD), f32),
            pltpu.VMEM((CHUNK, D_SSD), f32),
            pltpu.VMEM((CHUNK, LANES), f32),
            pltpu.VMEM((CHUNK, D_SSD), f32),
        ],
        compiler_params=_cparams(("arbitrary", "arbitrary")),
        name="ssd_sample" if sample else "ssd_prompt",
    )(proj, proj, dt_raw, cinit, h0, cw, cb, dtb, a_neg, dskip, ng, tri, expand)


def _rope_tables(pos, width):
    half = HD // 2
    inv = ROPE_THETA ** (-jnp.arange(half, dtype=f32) / half)
    ang = pos.astype(f32)[:, None] * inv[None, :]
    cos = jnp.cos(ang)
    sin = jnp.sin(ang)
    cos_h = jnp.concatenate([cos, cos], axis=-1)
    sin_h = jnp.concatenate([-sin, sin], axis=-1)
    reps = width // HD
    return jnp.tile(cos_h, (1, reps)), jnp.tile(sin_h, (1, reps))


def _reachable_cache(c):
    tail = c[:, :, MAX_WINDOW - TAIL:].reshape(DEPTH, DEC_BATCH, TAIL, D_ATT)
    strided = c.reshape(DEPTH, DEC_BATCH, MAX_WINDOW // DILATIONS[2], DILATIONS[2], H_A, HD)[:, :, :, :DEC_SEQ]
    return strided.reshape(DEPTH, DEC_BATCH, N_STRIDED, D_ATT), tail


def kernel(x_prompt, x_sample, cache_win_k, cache_win_v, state_ssd_conv, state_ssd, state_ffn_conv,
           norm1_g, w_in, q_norm_g, k_norm_g, ssd_conv_w, ssd_conv_b, ssd_dt_bias, ssd_a_log,
           ssd_d, ssd_norm_g, w_out, norm2_g, w_up, ffn_conv_w, ffn_conv_b, w_down):
    w_in_b = w_in.astype(bf16)
    w_dt_b = jnp.pad(w_in[:, :, D_PROJ:], ((0, 0), (0, 0), (0, LANES - H_S))).astype(bf16)
    w_out_b = w_out.astype(bf16)
    w_up_b = w_up.astype(bf16)
    w_down_b = w_down.astype(bf16)
    g1 = norm1_g[:, None, :]
    g2 = norm2_g[:, None, :]
    qg = jnp.tile(q_norm_g, (1, LANES // HD))[:, None, :]
    kg = jnp.tile(k_norm_g, (1, LANES // HD))[:, None, :]
    cb_ssd = ssd_conv_b[:, None, :]
    lane_pad = ((0, 0), (0, LANES - H_S))
    dtb = jnp.pad(ssd_dt_bias, lane_pad)[:, None, :]
    a_neg = jnp.pad(-jnp.exp(ssd_a_log.astype(f32)), lane_pad)[:, None, :]
    dskip = jnp.repeat(ssd_d, HD, axis=1)[:, None, :]
    ng = ssd_norm_g[:, None, :]
    cb_ffn = ffn_conv_b[:, None, :]

    idx = np.arange(LANES)
    e2 = jnp.asarray((idx[:, None] // HD == idx[None, :] // HD).astype(np.float32), dtype=bf16)
    tri = jnp.asarray((idx[None, :] <= idx[:, None]).astype(np.float32), dtype=bf16)
    expand = jnp.asarray((idx[:, None] == np.arange(D_SSD)[None, :] // HD).astype(np.float32), dtype=bf16)
    wmask = jnp.asarray(_sample_key_weights())
    bias_band, bias_first = (jnp.asarray(a) for a in _attn_biases())
    cos_p, sin_p = _rope_tables(jnp.arange(SEQ), LANES)
    cos_s, sin_s = _rope_tables(PAST_LEN + jnp.arange(S_STEPS), GRP)

    xp = x_prompt.reshape(BATCH * SEQ, D_MODEL)
    xs = jnp.pad(x_sample, ((0, 0), (0, S_STEPS - DEC_SEQ), (0, 0))).reshape(S_ROWS, D_MODEL)

    cks, ckt = _reachable_cache(cache_win_k)
    cvs, cvt = _reachable_cache(cache_win_v)
    ssd_ci_p = jnp.zeros((BATCH, SUBLANES, D_XBC), f32)
    ssd_ci_s = jnp.pad(state_ssd_conv, ((0, 0), (0, 0), (SUBLANES - (SSD_CONV - 1), 0), (0, 0)))
    h0_s = state_ssd.reshape(DEPTH, DEC_BATCH, D_SSD, SSD_STATE)
    h0_p = jnp.zeros((1, 1, D_SSD, SSD_STATE), f32)
    ffn_ci_p = jnp.zeros((BATCH, FFN_PAD, 2 * D_FF), f32)
    ffn_ci_s = jnp.pad(state_ffn_conv, ((0, 0), (0, 0), (SUBLANES - (FFN_CONV - 1), S_STEPS), (0, 0))
                       ).reshape(DEPTH, 1, DEC_BATCH * FFN_PAD, 2 * D_FF)

    tm_in, tm_p = 1024, 512
    k_all = v_all = None
    outs = {k: [] for k in ("ks", "vs", "cp", "cs", "hp", "hs", "fp", "fs")}
    for i in range(DEPTH):
        proj, dt_raw = _in_proj(xp, g1, w_in_b, w_dt_b, i, tm_in)
        proj3 = proj.reshape(BATCH, SEQ, D_PROJ)
        att, k_all, v_all = _attention_prompt(proj3, cos_p, sin_p, qg, kg, e2, bias_band, bias_first,
                                              i, k_all, v_all)
        ssd, h_last = _ssd(proj, dt_raw, ssd_ci_p, h0_p, ssd_conv_w, cb_ssd, dtb, a_neg, dskip, ng, tri,
                           expand, i, sample=False)
        x1 = _out_proj(att.reshape(BATCH * SEQ, D_ATT), ssd, w_out_b, xp, i, tm_p)
        xp, hs_g, hs_u = _ffn(x1, g2, w_up_b, ffn_conv_w, cb_ffn, w_down_b, ffn_ci_p, i, tm=tm_p,
                              tiles_per_seq=SEQ // tm_p, per_tile=False)
        outs["cp"].append(proj3[:, SEQ - (SSD_CONV - 1):, COL_XBC:COL_XBC + D_XBC])
        outs["hp"].append(h_last.reshape(BATCH, H_S, HD, SSD_STATE))
        last = slice(SEQ // tm_p - 1, None, SEQ // tm_p)
        hs = jnp.concatenate([hs_g[last], hs_u[last]], axis=-1)
        outs["fp"].append(hs[:, FFN_PAD - (FFN_CONV - 1):, :])

        proj_s, dt_s = _in_proj(xs, g1, w_in_b, w_dt_b, i, S_ROWS)
        att_s, kn_s = _attention_sample(proj_s, cks, ckt, cvs, cvt, cos_s, sin_s, qg, kg, e2, wmask, i)
        ssd_s, h_last_s = _ssd(proj_s, dt_s, ssd_ci_s[i], h0_s, ssd_conv_w, cb_ssd, dtb, a_neg, dskip, ng,
                               tri, expand, i, sample=True)
        x1_s = _out_proj(att_s, ssd_s, w_out_b, xs, i, S_ROWS)
        xs, hs_g, hs_u = _ffn(x1_s, g2, w_up_b, ffn_conv_w, cb_ffn, w_down_b, ffn_ci_s[i], i,
                              tm=S_ROWS, tiles_per_seq=1, per_tile=True)
        p3 = proj_s.reshape(DEC_BATCH, S_STEPS, D_PROJ)
        outs["ks"].append(kn_s.reshape(DEC_BATCH, S_STEPS, H_A, HD)[:, :DEC_SEQ])
        outs["vs"].append(p3[:, :DEC_SEQ, 2 * D_ATT:3 * D_ATT].reshape(DEC_BATCH, DEC_SEQ, H_A, HD))
        outs["cs"].append(p3[:, DEC_SEQ - (SSD_CONV - 1):DEC_SEQ, COL_XBC:COL_XBC + D_XBC])
        outs["hs"].append(h_last_s.reshape(DEC_BATCH, H_S, HD, SSD_STATE))
        hs = jnp.concatenate([hs_g[0], hs_u[0]], axis=-1).reshape(DEC_BATCH, FFN_PAD, 2 * D_FF)
        lo = SUBLANES + DEC_SEQ - (FFN_CONV - 1)
        outs["fs"].append(hs[:, lo:lo + FFN_CONV - 1, :])

    yp = xp.reshape(BATCH, SEQ, D_MODEL)
    ys = xs.reshape(DEC_BATCH, S_STEPS, D_MODEL)[:, :DEC_SEQ]
    st = lambda k: jnp.stack(outs[k])
    win_shape = (DEPTH, BATCH, SEQ, H_A, HD)
    return (yp, ys, k_all.reshape(win_shape), v_all.reshape(win_shape), st("ks"), st("vs"),
            st("cp"), st("cs"), st("hp"), st("hs"), st("fp"), st("fs"))
```

```python
import functools

import numpy as np
import jax
import jax.numpy as jnp
from jax import lax
from jax.experimental import pallas as pl
from jax.experimental.pallas import tpu as pltpu

f32 = jnp.float32
bf16 = jnp.bfloat16

D_MODEL = 2048
BATCH = 4
SEQ = 2048
DEPTH = 4
DEC_BATCH = 8
DEC_SEQ = 4
PAST_LEN = 16384
HD = 64
D_ATT = 1024
H_A = 16
DILATIONS = (1, 4, 16)
WIN_KEYS = 128
MAX_WINDOW = 2048
ROPE_THETA = 10000.0
D_SSD = 1024
H_S = 16
SSD_STATE = 128
SSD_GROUPS = 4
SSD_CONV = 4
D_XBC = 2048
D_FF = 5632
FFN_CONV = 3
EPS = 1e-6

LANES = 128
SUBLANES = 8
CHUNK = 128
S_STEPS = SUBLANES
S_ROWS = S_STEPS * DEC_BATCH

COL_Z = 3 * D_ATT
COL_XBC = COL_Z + D_SSD
D_PROJ = COL_XBC + D_XBC

VMEM_LIMIT = 56 * 1024 * 1024


def _cparams(sem):
    return pltpu.CompilerParams(dimension_semantics=sem, vmem_limit_bytes=VMEM_LIMIT)


def _dot(a, b):
    return jnp.dot(a, b, preferred_element_type=f32)


def _dot_nt(a, b):
    return lax.dot_general(a, b, (((1,), (1,)), ((), ())), preferred_element_type=f32)


def _split(x, parts):
    out = []
    rem = x
    for p in range(parts):
        hi = rem.astype(bf16)
        out.append(hi)
        if p + 1 < parts:
            rem = rem - hi.astype(f32)
    return out


def _split_dot(x, w, parts):
    acc = None
    for hi in _split(x, parts):
        t = _dot(hi, w)
        acc = t if acc is None else acc + t
    return acc


def _silu(x):
    return x * (1.0 / (1.0 + jnp.exp(-x)))


def _softplus(x):
    return jnp.maximum(x, 0.0) + jnp.log1p(jnp.exp(-jnp.abs(x)))


def _rmsnorm_to(x_ref, g_ref, xn_ref, rows):
    chunk = min(rows, 256)

    def body(i, c):
        r = pl.multiple_of(i * chunk, chunk)
        x = x_ref[pl.ds(r, chunk), :]
        ms = jnp.mean(x * x, axis=-1, keepdims=True)
        xn_ref[pl.ds(r, chunk), :] = (x * lax.rsqrt(ms + EPS) * g_ref[...]).astype(bf16)
        return c

    lax.fori_loop(0, rows // chunk, body, 0)


def _inproj_kernel(x_ref, g_ref, w_ref, wdt_ref, o_ref, dt_ref, xn_ref, *, tm):
    @pl.when(pl.program_id(1) == 0)
    def _():
        _rmsnorm_to(x_ref, g_ref, xn_ref, tm)
        dt_ref[...] = _dot(xn_ref[...], wdt_ref[...])

    o_ref[...] = _dot(xn_ref[...], w_ref[...])


def _in_proj(x, g, w, wdt, layer, tm, tn=1536):
    rows = x.shape[0]
    return pl.pallas_call(
        functools.partial(_inproj_kernel, tm=tm),
        grid=(rows // tm, D_PROJ // tn),
        in_specs=[
            pl.BlockSpec((tm, D_MODEL), lambda i, j: (i, 0)),
            pl.BlockSpec((None, 1, D_MODEL), lambda i, j: (layer, 0, 0)),
            pl.BlockSpec((None, D_MODEL, tn), lambda i, j: (layer, 0, j)),
            pl.BlockSpec((None, D_MODEL, LANES), lambda i, j: (layer, 0, 0)),
        ],
        out_specs=[pl.BlockSpec((tm, tn), lambda i, j: (i, j)),
                   pl.BlockSpec((tm, LANES), lambda i, j: (i, 0))],
        out_shape=[jax.ShapeDtypeStruct((rows, D_PROJ), f32),
                   jax.ShapeDtypeStruct((rows, LANES), f32)],
        scratch_shapes=[pltpu.VMEM((tm, D_MODEL), bf16)],
        compiler_params=_cparams(("arbitrary", "arbitrary")),
        name="in_proj",
    )(x, g, w, wdt)


def _outproj_kernel(a_ref, s_ref, wa_ref, ws_ref, x_ref, o_ref):
    acc = _dot(a_ref[...].astype(bf16), wa_ref[...])
    acc = acc + _dot(s_ref[...].astype(bf16), ws_ref[...])
    o_ref[...] = x_ref[...] + acc


def _out_proj(att, ssd, w, x, layer, tm, tn=D_MODEL):
    rows = x.shape[0]
    return pl.pallas_call(
        _outproj_kernel,
        grid=(rows // tm, D_MODEL // tn),
        in_specs=[
            pl.BlockSpec((tm, D_ATT), lambda i, j: (i, 0)),
            pl.BlockSpec((tm, D_SSD), lambda i, j: (i, 0)),
            pl.BlockSpec((None, D_ATT, tn), lambda i, j: (layer, 0, j)),
            pl.BlockSpec((None, D_SSD, tn), lambda i, j: (layer, 1, j)),
            pl.BlockSpec((tm, tn), lambda i, j: (i, j)),
        ],
        out_specs=pl.BlockSpec((tm, tn), lambda i, j: (i, j)),
        out_shape=jax.ShapeDtypeStruct((rows, D_MODEL), f32),
        compiler_params=_cparams(("arbitrary", "arbitrary")),
        name="out_proj",
    )(att, ssd, w, w, x)


FFN_PAD = 2 * SUBLANES


def _ffn_kernel(x_ref, g_ref, wg_ref, wu_ref, cwg_ref, cwu_ref, cbg_ref, cbu_ref, wd_ref,
                cig_ref, ciu_ref, o_ref, hsg_ref, hsu_ref,
                xn_ref, acc_ref, hbg_ref, hbu_ref, cag_ref, cau_ref, act_ref,
                *, tm, tiles_per_seq, per_tile, rc):
    i = pl.program_id(0)
    j = pl.program_id(1)
    nj = pl.num_programs(1)
    pad = FFN_PAD
    gate = (wg_ref, cig_ref, hbg_ref, cag_ref, hsg_ref, cwg_ref, cbg_ref)
    up = (wu_ref, ciu_ref, hbu_ref, cau_ref, hsu_ref, cwu_ref, cbu_ref)

    @pl.when(j == 0)
    def _():
        _rmsnorm_to(x_ref, g_ref, xn_ref, tm)
        acc_ref[...] = jnp.zeros_like(acc_ref)

    def conv_act(r, out_rows):
        convs = []
        for _, _, hb_ref, _, _, cw_ref, cb_ref in (gate, up):
            y = cb_ref[...] + hb_ref[r - 2:r - 2 + rc, :] * cw_ref[0:1, :]
            y = y + hb_ref[r - 1:r - 1 + rc, :] * cw_ref[1:2, :]
            y = y + hb_ref[r:r + rc, :] * cw_ref[2:3, :]
            convs.append(y)
        act_ref[out_rows, :] = (_silu(convs[0]) * convs[1]).astype(bf16)

    if per_tile:
        for w_ref, ci_ref, hb_ref, _, hs_ref, _, _ in (gate, up):
            hb_ref[...] = ci_ref[...]
            h = _dot(xn_ref[...], w_ref[...])
            for b in range(tm // SUBLANES):
                hb_ref[b * pad + SUBLANES:(b + 1) * pad, :] = h[b * SUBLANES:(b + 1) * SUBLANES, :]
            hs_ref[...] = hb_ref[...]
        for b in range(tm // SUBLANES):
            conv_act(b * pad + SUBLANES, slice(b * SUBLANES, (b + 1) * SUBLANES))
        acc_ref[...] += _dot(act_ref[...], wd_ref[...])
    else:
        first = (i % tiles_per_seq) == 0
        for _, ci_ref, hb_ref, ca_ref, _, _, _ in (gate, up):
            hb_ref[0:pad, :] = jnp.where(first, ci_ref[...], ca_ref[j])

        half = tm // 2
        halves = (slice(0, half), slice(half, tm))

        def up_piece(rows, branch):
            w_ref, hb_ref = branch[0], branch[2]
            hb_ref[pad + rows.start:pad + rows.stop, :] = _dot(xn_ref[rows, :], w_ref[...])

        def down_piece(rows, cols):
            acc_ref[rows, cols] += _dot(act_ref[rows, :], wd_ref[:, cols])

        def conv_rows(r0, n):
            for c in range(n):
                conv_act(pad + r0 + c * rc, slice(r0 + c * rc, r0 + (c + 1) * rc))

        per = half // rc // 2
        lo, hi = slice(0, D_MODEL // 2), slice(D_MODEL // 2, D_MODEL)
        up_piece(halves[0], gate)
        up_piece(halves[0], up)
        up_piece(halves[1], gate)
        conv_rows(0, per)
        up_piece(halves[1], up)
        conv_rows(per * rc, per)
        down_piece(halves[0], lo)
        conv_rows(half, per)
        down_piece(halves[0], hi)
        conv_rows(half + per * rc, per)
        down_piece(halves[1], lo)
        down_piece(halves[1], hi)
        for _, _, hb_ref, ca_ref, hs_ref, _, _ in (gate, up):
            ca_ref[j] = hb_ref[tm:tm + pad, :]
            hs_ref[...] = hb_ref[tm:tm + pad, :]

    @pl.when(j == nj - 1)
    def _():
        o_ref[...] = x_ref[...] + acc_ref[...]


def _ffn(x, g, w_up, cw, cb, w_down, cin, layer, *, tm, tiles_per_seq, per_tile, tf=512):
    rows = x.shape[0]
    nj = D_FF // tf
    pad = FFN_PAD
    rc = SUBLANES if per_tile else min(tm, 64)
    hb_rows = (tm // SUBLANES) * pad if per_tile else tm + pad
    st_rows = hb_rows if per_tile else pad
    kern = functools.partial(_ffn_kernel, tm=tm, tiles_per_seq=tiles_per_seq, per_tile=per_tile, rc=rc)
    return pl.pallas_call(
        kern,
        grid=(rows // tm, nj),
        in_specs=[
            pl.BlockSpec((tm, D_MODEL), lambda i, j: (i, 0)),
            pl.BlockSpec((None, 1, D_MODEL), lambda i, j: (layer, 0, 0)),
            pl.BlockSpec((None, D_MODEL, tf), lambda i, j: (layer, 0, j)),
            pl.BlockSpec((None, D_MODEL, tf), lambda i, j: (layer, 0, nj + j)),
            pl.BlockSpec((None, FFN_CONV, tf), lambda i, j: (layer, 0, j)),
            pl.BlockSpec((None, FFN_CONV, tf), lambda i, j: (layer, 0, nj + j)),
            pl.BlockSpec((None, 1, tf), lambda i, j: (layer, 0, j)),
            pl.BlockSpec((None, 1, tf), lambda i, j: (layer, 0, nj + j)),
            pl.BlockSpec((None, tf, D_MODEL), lambda i, j: (layer, j, 0)),
            pl.BlockSpec((None, st_rows, tf), lambda i, j: (i // tiles_per_seq, 0, j)),
            pl.BlockSpec((None, st_rows, tf), lambda i, j: (i // tiles_per_seq, 0, nj + j)),
        ],
        out_specs=[
            pl.BlockSpec((tm, D_MODEL), lambda i, j: (i, 0)),
            pl.BlockSpec((None, st_rows, tf), lambda i, j: (i, 0, j)),
            pl.BlockSpec((None, st_rows, tf), lambda i, j: (i, 0, j)),
        ],
        out_shape=[
            jax.ShapeDtypeStruct((rows, D_MODEL), f32),
            jax.ShapeDtypeStruct((rows // tm, st_rows, D_FF), f32),
            jax.ShapeDtypeStruct((rows // tm, st_rows, D_FF), f32),
        ],
        scratch_shapes=[
            pltpu.VMEM((tm, D_MODEL), bf16),
            pltpu.VMEM((tm, D_MODEL), f32),
            pltpu.VMEM((hb_rows, tf), f32),
            pltpu.VMEM((hb_rows, tf), f32),
            pltpu.VMEM((nj, pad, tf), f32),
            pltpu.VMEM((nj, pad, tf), f32),
            pltpu.VMEM((tm, tf), bf16),
        ],
        compiler_params=_cparams(("arbitrary", "arbitrary")),
        name="conv_ffn",
    )(x, g, w_up, w_up, cw, cw, cb, cb, w_down, cin, cin)


def _head_norm_rope(x, gain, cos, sin, e2, low_half):
    ssq = _split_dot(x * x, e2, 2)
    xn = x * lax.rsqrt(ssq * (1.0 / HD) + EPS) * gain
    rot = jnp.where(low_half, pltpu.roll(xn, LANES - HD // 2, 1), pltpu.roll(xn, HD // 2, 1))
    return xn * cos + rot * sin


def _attn_block(q_ref, k_ref, ve_ref, q_rows, k_rows, bias_ref, lane_lo):
    qb = q_ref[q_rows, :]
    zero = jnp.zeros_like(qb)
    q2 = jnp.concatenate([jnp.where(lane_lo, qb, zero), jnp.where(lane_lo, zero, qb)], axis=0)
    s = _dot_nt(q2, k_ref[k_rows, :]) + bias_ref[...]
    m = jnp.max(s, axis=1, keepdims=True)
    p = jnp.exp(s - m).astype(bf16)
    r = _dot(p, ve_ref[k_rows, :])
    shape = (CHUNK, LANES)
    o = jnp.where(lane_lo, r[:CHUNK, :LANES], r[CHUNK:, :LANES])
    le = jnp.where(lane_lo, r[:CHUNK, LANES:], r[CHUNK:, LANES:])
    me = jnp.where(lane_lo, jnp.broadcast_to(m[:CHUNK], shape), jnp.broadcast_to(m[CHUNK:], shape))
    return o, me, le


def _attn_kernel(*refs, aliased):
    (q_ref, k_ref, v_ref, cos_ref, sin_ref, qg_ref, kg_ref, e2_ref, bband_ref, bfirst_ref) = refs[:10]
    refs = refs[12:] if aliased else refs[10:]
    (att_ref, ko_ref, vo_ref, qf_ref, q4f_ref, k4f_ref, v4f_ref,
     q1_ref, k1_ref, v1_ref, q4_ref, k4_ref, v4_ref, q16_ref, k16_ref, v16_ref,
     o1_ref, m1_ref, l1_ref, o4_ref, m4_ref, l4_ref, o16_ref, m16_ref, l16_ref) = refs
    lane_lo = lax.broadcasted_iota(jnp.int32, (CHUNK, LANES), 1) < HD
    rows_a = 256
    lane_a = lax.broadcasted_iota(jnp.int32, (rows_a, LANES), 1)
    low_half_a = (lane_a % HD) < (HD // 2)

    @pl.when((pl.program_id(0) == 0) & (pl.program_id(1) == 0))
    def _():
        ones = jnp.ones((SEQ, LANES), bf16)
        v1_ref[:, LANES:] = ones
        v4_ref[:, LANES:] = ones
        v16_ref[:, LANES:] = ones

    def prep(i, c):
        r = pl.multiple_of(i * rows_a, rows_a)
        sl = pl.ds(r, rows_a)
        cos = cos_ref[sl, :]
        sin = sin_ref[sl, :]
        q = _head_norm_rope(q_ref[sl, :], qg_ref[...], cos, sin, e2_ref[...], low_half_a) * (HD ** -0.5)
        k = _head_norm_rope(k_ref[sl, :], kg_ref[...], cos, sin, e2_ref[...], low_half_a)
        v = v_ref[sl, :]
        qf_ref[sl, :] = q
        ko_ref[sl, :] = k
        vo_ref[sl, :] = v
        q1_ref[sl, :] = q.astype(bf16)
        k1_ref[sl, :] = k.astype(bf16)
        v1_ref[sl, 0:LANES] = v.astype(bf16)
        return c

    lax.fori_loop(0, SEQ // rows_a, prep, 0, unroll=2)

    quarter = SEQ // 4
    streams = ((qf_ref, q4f_ref, q4_ref, q16_ref), (ko_ref, k4f_ref, k4_ref, k16_ref),
               (v_ref, v4f_ref, v4_ref, v16_ref))
    for r in range(4):
        src = pl.ds(r, quarter, stride=4)
        dst = pl.ds(r * quarter, quarter)
        for tok_ref, d4f_ref, d4_ref, _ in streams:
            x = tok_ref[src, :]
            d4f_ref[dst, :] = x
            d4_ref[dst, 0:LANES] = x.astype(bf16)
    for r16 in range(16):
        src = pl.ds((r16 % 4) * quarter + r16 // 4, CHUNK, stride=4)
        dst = pl.ds(r16 * CHUNK, CHUNK)
        for _, d4f_ref, _, d16_ref in streams:
            d16_ref[dst, 0:LANES] = d4f_ref[src, :].astype(bf16)

    def first_block(qd, kd, vd, outs, base, dst):
        rows = pl.ds(base, CHUNK)
        res = _attn_block(qd, kd, vd, rows, rows, bfirst_ref, lane_lo)
        for ref, val in zip(outs, res):
            ref[dst, :] = val

    def band_block(qd, kd, vd, outs, q0, dst):
        k_rows = pl.ds(q0 - CHUNK, 2 * CHUNK)
        res = _attn_block(qd, kd, vd, pl.ds(q0, CHUNK), k_rows, bband_ref, lane_lo)
        for ref, val in zip(outs, res):
            ref[dst, :] = val

    p1 = (q1_ref, k1_ref, v1_ref, (o1_ref, m1_ref, l1_ref))
    p4 = (q4_ref, k4_ref, v4_ref, (o4_ref, m4_ref, l4_ref))
    p16 = (q16_ref, k16_ref, v16_ref, (o16_ref, m16_ref, l16_ref))

    first_block(*p1, 0, pl.ds(0, CHUNK))
    for r in range(4):
        first_block(*p4, r * quarter, pl.ds(r * quarter, CHUNK))

    for c in range(1, SEQ // CHUNK):
        band_block(*p1, c * CHUNK, pl.ds(c * CHUNK, CHUNK))
    for r in range(4):
        for c in range(1, quarter // CHUNK):
            q0 = r * quarter + c * CHUNK
            band_block(*p4, q0, pl.ds(q0, CHUNK))
    for r16 in range(16):
        dst = pl.ds((r16 % 4) * quarter + r16 // 4, CHUNK, stride=4)
        first_block(*p16, r16 * CHUNK, dst)

    for r in range(4):
        def fin(mb, c, r=r):
            sl = pl.ds(pl.multiple_of(r * quarter + mb * CHUNK, CHUNK), CHUNK)
            tok = pl.ds(r + 4 * CHUNK * mb, CHUNK, stride=4)
            m1, m4, m16 = m1_ref[tok, :], m4_ref[sl, :], m16_ref[sl, :]
            mx = jnp.maximum(jnp.maximum(m1, m4), m16)
            a1, a4, a16 = jnp.exp(m1 - mx), jnp.exp(m4 - mx), jnp.exp(m16 - mx)
            num = o1_ref[tok, :] * a1 + o4_ref[sl, :] * a4 + o16_ref[sl, :] * a16
            den = l1_ref[tok, :] * a1 + l4_ref[sl, :] * a4 + l16_ref[sl, :] * a16
            att_ref[tok, :] = num / den
            return c

        lax.fori_loop(0, quarter // CHUNK, fin, 0, unroll=2)


def _attention_prompt(proj3, cos, sin, qg, kg, e2, bias_band, bias_first, layer, k_all, v_all):
    nhp = D_ATT // LANES
    blk = lambda off: pl.BlockSpec((None, SEQ, LANES), lambda b, h: (b, 0, off + h))
    tab = pl.BlockSpec((SEQ, LANES), lambda b, h: (0, 0))
    gain = pl.BlockSpec((None, 1, LANES), lambda b, h: (layer, 0, 0))
    const = lambda shape: pl.BlockSpec(shape, lambda b, h: (0, 0))
    stacked = pl.BlockSpec((None, None, SEQ, LANES), lambda b, h: (layer, b, 0, h))
    bscr = lambda w: pltpu.VMEM((SEQ, w), bf16)
    fscr = lambda: pltpu.VMEM((SEQ, LANES), f32)
    aliased = k_all is not None
    in_specs = [blk(0), blk(nhp), blk(2 * nhp), tab, tab, gain, gain, const((LANES, LANES)),
                const((2 * CHUNK, 2 * CHUNK)), const((2 * CHUNK, CHUNK))]
    args = [proj3, proj3, proj3, cos, sin, qg, kg, e2, bias_band, bias_first]
    aliases = {}
    if aliased:
        in_specs += [pl.BlockSpec(memory_space=pl.ANY)] * 2
        aliases = {len(args): 1, len(args) + 1: 2}
        args += [k_all, v_all]
    return pl.pallas_call(
        functools.partial(_attn_kernel, aliased=aliased),
        grid=(BATCH, nhp),
        in_specs=in_specs,
        out_specs=[pl.BlockSpec((None, SEQ, LANES), lambda b, h: (b, 0, h)), stacked, stacked],
        out_shape=[jax.ShapeDtypeStruct((BATCH, SEQ, D_ATT), f32),
                   jax.ShapeDtypeStruct((DEPTH, BATCH, SEQ, D_ATT), f32),
                   jax.ShapeDtypeStruct((DEPTH, BATCH, SEQ, D_ATT), f32)],
        scratch_shapes=([fscr() for _ in range(4)] + [bscr(LANES), bscr(LANES), bscr(2 * LANES)] * 3
                        + [fscr() for _ in range(9)]),
        input_output_aliases=aliases,
        compiler_params=_cparams(("arbitrary", "arbitrary")),
        name="attn_prompt",
    )(*args)


def _attn_biases():
    qi = (np.arange(2 * CHUNK) % CHUNK)[:, None]
    ki = np.arange(2 * CHUNK)[None, :]
    band = np.where((ki >= qi) & (ki <= qi + CHUNK), 0.0, -np.inf).astype(np.float32)
    first = np.where(ki[:, :CHUNK] <= qi, 0.0, -np.inf).astype(np.float32)
    return band, first


GRP = 4 * HD
N_CACHED = MAX_WINDOW
KPAD = N_CACHED + LANES


def _attn_sample_kernel(q_ref, k_ref, v_ref, ck_ref, cv_ref, cos_ref, sin_ref,
                        qg_ref, kg_ref, e2_ref, w_ref, att_ref, kn_ref, kc_ref, vc_ref):
    lane = lax.broadcasted_iota(jnp.int32, (S_STEPS, LANES), 1)
    low_half = (lane % HD) < (HD // 2)
    e2 = e2_ref[...]
    qs, ks = [], []
    for t in range(GRP // LANES):
        cs = slice(t * LANES, (t + 1) * LANES)
        cos = cos_ref[:, cs]
        sin = sin_ref[:, cs]
        qs.append(_head_norm_rope(q_ref[:, cs], qg_ref[...], cos, sin, e2, low_half) * (HD ** -0.5))
        ks.append(_head_norm_rope(k_ref[:, cs], kg_ref[...], cos, sin, e2, low_half))
    q = jnp.concatenate(qs, axis=1)
    k = jnp.concatenate(ks, axis=1)
    kn_ref[...] = k

    kc_ref[0:N_CACHED, :] = ck_ref[...].astype(bf16)
    vc_ref[0:N_CACHED, :] = cv_ref[...].astype(bf16)
    kc_ref[N_CACHED:KPAD, :] = jnp.zeros((KPAD - N_CACHED, GRP), bf16)
    vc_ref[N_CACHED:KPAD, :] = jnp.zeros((KPAD - N_CACHED, GRP), bf16)
    kc_ref[N_CACHED:N_CACHED + 2 * S_STEPS, :] = jnp.concatenate(
        [k, jnp.zeros_like(k)], axis=0).astype(bf16)
    vc_ref[N_CACHED:N_CACHED + 2 * S_STEPS, :] = jnp.concatenate(
        [v_ref[...], jnp.zeros_like(k)], axis=0).astype(bf16)

    nq = (GRP // HD) * S_STEPS
    qt = jnp.concatenate([q] * (GRP // HD) + [jnp.zeros((LANES - nq, GRP), f32)], axis=0)
    row_h = lax.broadcasted_iota(jnp.int32, (LANES, GRP), 0) // S_STEPS
    lane_h = lax.broadcasted_iota(jnp.int32, (LANES, GRP), 1) // HD
    same_head = row_h == lane_h
    qt = jnp.where(same_head, qt, 0.0).astype(bf16)
    s = _dot_nt(kc_ref[...], qt)
    w = w_ref[...]
    keep = w > 0.0
    m = jnp.max(jnp.where(keep, s, -jnp.inf), axis=0, keepdims=True)
    e = jnp.where(keep, w * jnp.exp(s - m), 0.0)
    den = jnp.sum(e, axis=0, keepdims=True)
    pt = jnp.transpose(e / den).astype(bf16)
    res = _dot(pt, vc_ref[...])
    res = jnp.where(same_head, res, 0.0)
    out = res[0:S_STEPS]
    for h in range(1, GRP // HD):
        out = out + res[h * S_STEPS:(h + 1) * S_STEPS]
    att_ref[...] = out


def _attention_sample(proj, ck, cv, cos, sin, qg, kg, e2, wmask, layer):
    ng = D_ATT // GRP
    col = lambda off: pl.BlockSpec((S_STEPS, GRP), lambda b, g: (b, off + g))
    cache = pl.BlockSpec((None, None, N_CACHED, GRP), lambda b, g: (layer, b, 0, g))
    tab = pl.BlockSpec((S_STEPS, GRP), lambda b, g: (0, 0))
    gain = pl.BlockSpec((None, 1, LANES), lambda b, g: (layer, 0, 0))
    return pl.pallas_call(
        _attn_sample_kernel,
        grid=(DEC_BATCH, ng),
        in_specs=[col(0), col(ng), col(2 * ng), cache, cache, tab, tab, gain, gain,
                  pl.BlockSpec((LANES, LANES), lambda b, g: (0, 0)),
                  pl.BlockSpec((KPAD, LANES), lambda b, g: (0, 0))],
        out_specs=[pl.BlockSpec((S_STEPS, GRP), lambda b, g: (b, g))] * 2,
        out_shape=[jax.ShapeDtypeStruct((S_ROWS, D_ATT), f32)] * 2,
        scratch_shapes=[pltpu.VMEM((KPAD, GRP), bf16), pltpu.VMEM((KPAD, GRP), bf16)],
        compiler_params=_cparams(("arbitrary", "arbitrary")),
        name="attn_sample",
    )(proj, proj, proj, ck, cv, cos, sin, qg, kg, e2, wmask)


def _sample_key_weights():
    pos = np.full((KPAD,), -1, np.int64)
    pos[:N_CACHED + DEC_SEQ] = np.arange(N_CACHED + DEC_SEQ)
    w = np.zeros((KPAD, LANES), np.float32)
    for t in range(S_STEPS):
        mult = np.zeros((KPAD,), np.float32)
        if t < DEC_SEQ:
            dist = np.where(pos >= 0, MAX_WINDOW + t - pos, -1)
            for d in DILATIONS:
                mult += ((dist >= 0) & (dist % d == 0) & (dist <= WIN_KEYS * d)).astype(np.float32)
        else:
            mult[N_CACHED + t] = 1.0
        for h in range(GRP // HD):
            w[:, h * S_STEPS + t] = mult
    w[0, (GRP // HD) * S_STEPS:] = 1.0
    return w


def _ssd_kernel(z_ref, xbc_ref, dt_ref, ci_ref, h0_ref, cw_ref, cb_ref, dtb_ref, a_ref, dsk_ref,
                ng_ref, tri_ref, exp_ref, y_ref, ho_ref,
                cbuf_ref, xc_ref, st_ref, zb_ref, db_ref, yb_ref,
                *, padded, valid, has_h0):
    c = pl.program_id(1)
    nc = pl.num_programs(1)

    if padded:
        zb_ref[...] = jnp.zeros_like(zb_ref)
        db_ref[...] = jnp.zeros_like(db_ref)
        cbuf_ref[SUBLANES:, :] = jnp.zeros((CHUNK, D_XBC), f32)
        zb_ref[0:S_STEPS, :] = z_ref[...]
        db_ref[0:S_STEPS, :] = dt_ref[...]
        cbuf_ref[SUBLANES:SUBLANES + S_STEPS, :] = xbc_ref[...]
        zsrc, dsrc = zb_ref, db_ref
    else:
        cbuf_ref[SUBLANES:, :] = xbc_ref[...]
        zsrc, dsrc = z_ref, dt_ref

    @pl.when(c == 0)
    def _():
        cbuf_ref[0:SUBLANES, :] = ci_ref[...]
        if has_h0:
            st_ref[...] = jnp.transpose(h0_ref[...])
        else:
            st_ref[...] = jnp.zeros_like(st_ref)

    for t in range(D_XBC // 256):
        cs = slice(t * 256, (t + 1) * 256)
        acc = cb_ref[:, cs] + cbuf_ref[5:5 + CHUNK, cs] * cw_ref[0:1, cs]
        acc = acc + cbuf_ref[6:6 + CHUNK, cs] * cw_ref[1:2, cs]
        acc = acc + cbuf_ref[7:7 + CHUNK, cs] * cw_ref[2:3, cs]
        acc = acc + cbuf_ref[8:8 + CHUNK, cs] * cw_ref[3:4, cs]
        xc_ref[:, cs] = _silu(acc)
    cbuf_ref[0:SUBLANES, :] = cbuf_ref[CHUNK:CHUNK + SUBLANES, :]

    tri = tri_ref[...]
    expand = exp_ref[...]
    dt = _softplus(dsrc[...] + dtb_ref[...])
    a = dt * a_ref[...]
    a_cum = None
    for hi in _split(a, 3):
        t_ = _dot(tri, hi)
        a_cum = t_ if a_cum is None else a_cum + t_
    a_cum_t = jnp.transpose(a_cum)
    dt_e = _split_dot(dt, expand, 2)
    acum_e = _split_dot(a_cum, expand, 3)
    alast_e = acum_e[valid - 1:valid, :]
    row = lax.broadcasted_iota(jnp.int32, (CHUNK, D_SSD), 0)

    xs = xc_ref[:, 0:D_SSD]
    xdt = xs * dt_e
    xdt_b = xdt.astype(bf16)
    xend_b = jnp.where(row < valid, xdt * jnp.exp(alast_e - acum_e), 0.0).astype(bf16)

    ii = lax.broadcasted_iota(jnp.int32, (CHUNK, CHUNK), 0)
    jj = lax.broadcasted_iota(jnp.int32, (CHUNK, CHUNK), 1)
    causal = jj <= ii
    lane_lo = lax.broadcasted_iota(jnp.int32, (CHUNK, LANES), 1) < HD
    hpg = H_S // SSD_GROUPS
    gw = hpg * HD
    for g in range(SSD_GROUPS):
        gs = slice(g * gw, (g + 1) * gw)
        b0 = D_SSD + g * SSD_STATE
        c0 = D_SSD + (SSD_GROUPS + g) * SSD_STATE
        bm = xc_ref[:, b0:b0 + SSD_STATE]
        cm = xc_ref[:, c0:c0 + SSD_STATE].astype(bf16)
        cbm = _dot_nt(cm, bm.astype(bf16))
        bt = jnp.transpose(bm).astype(bf16)
        st_new = _dot(bt, xend_b[:, gs])
        y_off = _dot(cm, st_ref[:, gs].astype(bf16))
        for pr in range(hpg // 2):
            ys = []
            ps = slice(g * gw + pr * LANES, g * gw + (pr + 1) * LANES)
            for hh in range(2):
                h = g * hpg + pr * 2 + hh
                seg = a_cum[:, h:h + 1] - a_cum_t[h:h + 1, :]
                gm = (cbm * jnp.exp(jnp.where(causal, seg, -jnp.inf))).astype(bf16)
                ys.append(_dot(gm, xdt_b[:, ps]))
            yb_ref[:, ps] = jnp.where(lane_lo, ys[0], ys[1])
        yb_ref[:, gs] = yb_ref[:, gs] + y_off * jnp.exp(acum_e[:, gs])
        st_ref[:, gs] = jnp.exp(alast_e[:, gs]) * st_ref[:, gs] + st_new

    y = yb_ref[...] + dsk_ref[...] * xs
    y = y * _silu(zsrc[...])
    ms = jnp.mean(y * y, axis=-1, keepdims=True)
    y = y * lax.rsqrt(ms + EPS) * ng_ref[...]
    if padded:
        y_ref[...] = y[0:S_STEPS]
    else:
        y_ref[...] = y.astype(y_ref.dtype)

    @pl.when(c == nc - 1)
    def _():
        ho_ref[...] = jnp.transpose(st_ref[...])


def _ssd(proj, dt_raw, cinit, h0, cw, cb, dtb, a_neg, dskip, ng, tri, expand, layer, *, sample):
    if sample:
        nb, nc, rows = DEC_BATCH, 1, S_STEPS
        y_dtype = f32
        h0_spec = pl.BlockSpec((None, None, D_SSD, SSD_STATE), lambda b, c: (layer, b, 0, 0))
    else:
        nb, nc, rows = BATCH, SEQ // CHUNK, CHUNK
        y_dtype = bf16
        h0_spec = pl.BlockSpec((None, None, D_SSD, SSD_STATE), lambda b, c: (0, 0, 0, 0))
    rowblk = lambda width, colblk: pl.BlockSpec((rows, width), lambda b, c: (b * nc + c, colblk))
    vec = lambda width: pl.BlockSpec((None, 1, width), lambda b, c: (layer, 0, 0))
    const = lambda shape: pl.BlockSpec(shape, lambda b, c: (0, 0))
    kern = functools.partial(_ssd_kernel, padded=sample, valid=DEC_SEQ if sample else CHUNK,
                             has_h0=sample)
    return pl.pallas_call(
        kern,
        grid=(nb, nc),
        in_specs=[
            rowblk(D_SSD, COL_Z // D_SSD),
            rowblk(D_XBC, COL_XBC // D_XBC),
            rowblk(LANES, 0),
            pl.BlockSpec((None, SUBLANES, D_XBC), lambda b, c: (b, 0, 0)),
            h0_spec,
            pl.BlockSpec((None, SSD_CONV, D_XBC), lambda b, c: (layer, 0, 0)),
            vec(D_XBC), vec(LANES), vec(LANES), vec(D_SSD), vec(D_SSD),
            const((CHUNK, CHUNK)), const((LANES, D_SSD)),
        ],
        out_specs=[rowblk(D_SSD, 0), pl.BlockSpec((None, D_SSD, SSD_STATE), lambda b, c: (b, 0, 0))],
        out_shape=[jax.ShapeDtypeStruct((nb * nc * rows, D_SSD), y_dtype),
                   jax.ShapeDtypeStruct((nb, D_SSD, SSD_STATE), f32)],
        scratch_shapes=[
            pltpu.VMEM((CHUNK + SUBLANES, D_XBC), f32),
            pltpu.VMEM((CHUNK, D_XBC), f32),
            pltpu.VMEM((SSD_STATE, D_SSD), f32),
            pltpu.VMEM((CHUNK, D_SSD), f32),
            pltpu.VMEM((CHUNK, LANES), f32),
            pltpu.VMEM((CHUNK, D_SSD), f32),
        ],
        compiler_params=_cparams(("arbitrary", "arbitrary")),
        name="ssd_sample" if sample else "ssd_prompt",
    )(proj, proj, dt_raw, cinit, h0, cw, cb, dtb, a_neg, dskip, ng, tri, expand)


def _rope_tables(pos, width):
    half = HD // 2
    inv = ROPE_THETA ** (-jnp.arange(half, dtype=f32) / half)
    ang = pos.astype(f32)[:, None] * inv[None, :]
    cos = jnp.cos(ang)
    sin = jnp.sin(ang)
    cos_h = jnp.concatenate([cos, cos], axis=-1)
    sin_h = jnp.concatenate([-sin, sin], axis=-1)
    reps = width // HD
    return jnp.tile(cos_h, (1, reps)), jnp.tile(sin_h, (1, reps))


def kernel(x_prompt, x_sample, cache_win_k, cache_win_v, state_ssd_conv, state_ssd, state_ffn_conv,
           norm1_g, w_in, q_norm_g, k_norm_g, ssd_conv_w, ssd_conv_b, ssd_dt_bias, ssd_a_log,
           ssd_d, ssd_norm_g, w_out, norm2_g, w_up, ffn_conv_w, ffn_conv_b, w_down):
    w_in_b = w_in.astype(bf16)
    w_dt_b = jnp.pad(w_in[:, :, D_PROJ:], ((0, 0), (0, 0), (0, LANES - H_S))).astype(bf16)
    w_out_b = w_out.astype(bf16)
    w_up_b = w_up.astype(bf16)
    w_down_b = w_down.astype(bf16)
    g1 = norm1_g[:, None, :]
    g2 = norm2_g[:, None, :]
    qg = jnp.tile(q_norm_g, (1, LANES // HD))[:, None, :]
    kg = jnp.tile(k_norm_g, (1, LANES // HD))[:, None, :]
    cb_ssd = ssd_conv_b[:, None, :]
    lane_pad = ((0, 0), (0, LANES - H_S))
    dtb = jnp.pad(ssd_dt_bias, lane_pad)[:, None, :]
    a_neg = jnp.pad(-jnp.exp(ssd_a_log.astype(f32)), lane_pad)[:, None, :]
    dskip = jnp.repeat(ssd_d, HD, axis=1)[:, None, :]
    ng = ssd_norm_g[:, None, :]
    cb_ffn = ffn_conv_b[:, None, :]

    idx = np.arange(LANES)
    e2 = jnp.asarray((idx[:, None] // HD == idx[None, :] // HD).astype(np.float32), dtype=bf16)
    tri = jnp.asarray((idx[None, :] <= idx[:, None]).astype(np.float32), dtype=bf16)
    expand = jnp.asarray((idx[:, None] == np.arange(D_SSD)[None, :] // HD).astype(np.float32), dtype=bf16)
    wmask = jnp.asarray(_sample_key_weights())
    bias_band, bias_first = (jnp.asarray(a) for a in _attn_biases())
    cos_p, sin_p = _rope_tables(jnp.arange(SEQ), LANES)
    cos_s, sin_s = _rope_tables(PAST_LEN + jnp.arange(S_STEPS), GRP)

    xp = x_prompt.reshape(BATCH * SEQ, D_MODEL)
    xs = jnp.pad(x_sample, ((0, 0), (0, S_STEPS - DEC_SEQ), (0, 0))).reshape(S_ROWS, D_MODEL)

    ck = cache_win_k.reshape(DEPTH, DEC_BATCH, MAX_WINDOW, D_ATT)
    cv = cache_win_v.reshape(DEPTH, DEC_BATCH, MAX_WINDOW, D_ATT)
    ssd_ci_p = jnp.zeros((BATCH, SUBLANES, D_XBC), f32)
    ssd_ci_s = jnp.pad(state_ssd_conv, ((0, 0), (0, 0), (SUBLANES - (SSD_CONV - 1), 0), (0, 0)))
    h0_s = state_ssd.reshape(DEPTH, DEC_BATCH, D_SSD, SSD_STATE)
    h0_p = jnp.zeros((1, 1, D_SSD, SSD_STATE), f32)
    ffn_ci_p = jnp.zeros((BATCH, FFN_PAD, 2 * D_FF), f32)
    ffn_ci_s = jnp.pad(state_ffn_conv, ((0, 0), (0, 0), (SUBLANES - (FFN_CONV - 1), S_STEPS), (0, 0))
                       ).reshape(DEPTH, 1, DEC_BATCH * FFN_PAD, 2 * D_FF)

    tm_in, tm_p = 1024, 512
    k_all = v_all = None
    outs = {k: [] for k in ("ks", "vs", "cp", "cs", "hp", "hs", "fp", "fs")}
    for i in range(DEPTH):
        proj, dt_raw = _in_proj(xp, g1, w_in_b, w_dt_b, i, tm_in)
        proj3 = proj.reshape(BATCH, SEQ, D_PROJ)
        att, k_all, v_all = _attention_prompt(proj3, cos_p, sin_p, qg, kg, e2, bias_band, bias_first,
                                              i, k_all, v_all)
        ssd, h_last = _ssd(proj, dt_raw, ssd_ci_p, h0_p, ssd_conv_w, cb_ssd, dtb, a_neg, dskip, ng, tri,
                           expand, i, sample=False)
        x1 = _out_proj(att.reshape(BATCH * SEQ, D_ATT), ssd, w_out_b, xp, i, tm_p)
        xp, hs_g, hs_u = _ffn(x1, g2, w_up_b, ffn_conv_w, cb_ffn, w_down_b, ffn_ci_p, i, tm=tm_p,
                              tiles_per_seq=SEQ // tm_p, per_tile=False)
        outs["cp"].append(proj3[:, SEQ - (SSD_CONV - 1):, COL_XBC:COL_XBC + D_XBC])
        outs["hp"].append(h_last.reshape(BATCH, H_S, HD, SSD_STATE))
        last = slice(SEQ // tm_p - 1, None, SEQ // tm_p)
        hs = jnp.concatenate([hs_g[last], hs_u[last]], axis=-1)
        outs["fp"].append(hs[:, FFN_PAD - (FFN_CONV - 1):, :])

        proj_s, dt_s = _in_proj(xs, g1, w_in_b, w_dt_b, i, S_ROWS)
        att_s, kn_s = _attention_sample(proj_s, ck, cv, cos_s, sin_s, qg, kg, e2, wmask, i)
        ssd_s, h_last_s = _ssd(proj_s, dt_s, ssd_ci_s[i], h0_s, ssd_conv_w, cb_ssd, dtb, a_neg, dskip, ng,
                               tri, expand, i, sample=True)
        x1_s = _out_proj(att_s, ssd_s, w_out_b, xs, i, S_ROWS)
        xs, hs_g, hs_u = _ffn(x1_s, g2, w_up_b, ffn_conv_w, cb_ffn, w_down_b, ffn_ci_s[i], i,
                              tm=S_ROWS, tiles_per_seq=1, per_tile=True)
        p3 = proj_s.reshape(DEC_BATCH, S_STEPS, D_PROJ)
        outs["ks"].append(kn_s.reshape(DEC_BATCH, S_STEPS, H_A, HD)[:, :DEC_SEQ])
        outs["vs"].append(p3[:, :DEC_SEQ, 2 * D_ATT:3 * D_ATT].reshape(DEC_BATCH, DEC_SEQ, H_A, HD))
        outs["cs"].append(p3[:, DEC_SEQ - (SSD_CONV - 1):DEC_SEQ, COL_XBC:COL_XBC + D_XBC])
        outs["hs"].append(h_last_s.reshape(DEC_BATCH, H_S, HD, SSD_STATE))
        hs = jnp.concatenate([hs_g[0], hs_u[0]], axis=-1).reshape(DEC_BATCH, FFN_PAD, 2 * D_FF)
        lo = SUBLANES + DEC_SEQ - (FFN_CONV - 1)
        outs["fs"].append(hs[:, lo:lo + FFN_CONV - 1, :])

    yp = xp.reshape(BATCH, SEQ, D_MODEL)
    ys = xs.reshape(DEC_BATCH, S_STEPS, D_MODEL)[:, :DEC_SEQ]
    st = lambda k: jnp.stack(outs[k])
    win_shape = (DEPTH, BATCH, SEQ, H_A, HD)
    return (yp, ys, k_all.reshape(win_shape), v_all.reshape(win_shape), st("ks"), st("vs"),
            st("cp"), st("cs"), st("hp"), st("hs"), st("fp"), st("fs"))
```

```python
import functools

import numpy as np
import jax
import jax.numpy as jnp
from jax import lax
from jax.experimental import pallas as pl
from jax.experimental.pallas import tpu as pltpu

f32 = jnp.float32
bf16 = jnp.bfloat16

D_MODEL = 2048
BATCH = 4
SEQ = 2048
DEPTH = 4
DEC_BATCH = 8
DEC_SEQ = 4
PAST_LEN = 16384
HD = 64
D_ATT = 1024
H_A = 16
DILATIONS = (1, 4, 16)
WIN_KEYS = 128
MAX_WINDOW = 2048
ROPE_THETA = 10000.0
D_SSD = 1024
H_S = 16
SSD_STATE = 128
SSD_GROUPS = 4
SSD_CONV = 4
D_XBC = 2048
D_FF = 5632
FFN_CONV = 3
EPS = 1e-6

LANES = 128
SUBLANES = 8
CHUNK = 128
S_STEPS = SUBLANES
S_ROWS = S_STEPS * DEC_BATCH

COL_Z = 3 * D_ATT
COL_XBC = COL_Z + D_SSD
D_PROJ = COL_XBC + D_XBC

VMEM_LIMIT = 56 * 1024 * 1024


def _cparams(sem):
    return pltpu.CompilerParams(dimension_semantics=sem, vmem_limit_bytes=VMEM_LIMIT)


def _dot(a, b):
    return jnp.dot(a, b, preferred_element_type=f32)


def _dot_nt(a, b):
    return lax.dot_general(a, b, (((1,), (1,)), ((), ())), preferred_element_type=f32)


def _split(x, parts):
    out = []
    rem = x
    for p in range(parts):
        hi = rem.astype(bf16)
        out.append(hi)
        if p + 1 < parts:
            rem = rem - hi.astype(f32)
    return out


def _split_dot(x, w, parts):
    acc = None
    for hi in _split(x, parts):
        t = _dot(hi, w)
        acc = t if acc is None else acc + t
    return acc


def _silu(x):
    return x * (1.0 / (1.0 + jnp.exp(-x)))


def _softplus(x):
    return jnp.maximum(x, 0.0) + jnp.log1p(jnp.exp(-jnp.abs(x)))


def _rmsnorm_to(x_ref, g_ref, xn_ref, rows):
    chunk = min(rows, 256)

    def body(i, c):
        r = pl.multiple_of(i * chunk, chunk)
        x = x_ref[pl.ds(r, chunk), :]
        ms = jnp.mean(x * x, axis=-1, keepdims=True)
        xn_ref[pl.ds(r, chunk), :] = (x * lax.rsqrt(ms + EPS) * g_ref[...]).astype(bf16)
        return c

    lax.fori_loop(0, rows // chunk, body, 0)


def _inproj_kernel(x_ref, g_ref, w_ref, wdt_ref, o_ref, dt_ref, xn_ref, *, tm):
    @pl.when(pl.program_id(1) == 0)
    def _():
        _rmsnorm_to(x_ref, g_ref, xn_ref, tm)
        dt_ref[...] = _dot(xn_ref[...], wdt_ref[...])

    o_ref[...] = _dot(xn_ref[...], w_ref[...])


def _in_proj(x, g, w, wdt, layer, tm, tn=1536):
    rows = x.shape[0]
    return pl.pallas_call(
        functools.partial(_inproj_kernel, tm=tm),
        grid=(rows // tm, D_PROJ // tn),
        in_specs=[
            pl.BlockSpec((tm, D_MODEL), lambda i, j: (i, 0)),
            pl.BlockSpec((None, 1, D_MODEL), lambda i, j: (layer, 0, 0)),
            pl.BlockSpec((None, D_MODEL, tn), lambda i, j: (layer, 0, j)),
            pl.BlockSpec((None, D_MODEL, LANES), lambda i, j: (layer, 0, 0)),
        ],
        out_specs=[pl.BlockSpec((tm, tn), lambda i, j: (i, j)),
                   pl.BlockSpec((tm, LANES), lambda i, j: (i, 0))],
        out_shape=[jax.ShapeDtypeStruct((rows, D_PROJ), f32),
                   jax.ShapeDtypeStruct((rows, LANES), f32)],
        scratch_shapes=[pltpu.VMEM((tm, D_MODEL), bf16)],
        compiler_params=_cparams(("arbitrary", "arbitrary")),
        name="in_proj",
    )(x, g, w, wdt)


def _outproj_kernel(a_ref, s_ref, wa_ref, ws_ref, x_ref, o_ref):
    acc = _dot(a_ref[...].astype(bf16), wa_ref[...])
    acc = acc + _dot(s_ref[...].astype(bf16), ws_ref[...])
    o_ref[...] = x_ref[...] + acc


def _out_proj(att, ssd, w, x, layer, tm, tn=D_MODEL):
    rows = x.shape[0]
    return pl.pallas_call(
        _outproj_kernel,
        grid=(rows // tm, D_MODEL // tn),
        in_specs=[
            pl.BlockSpec((tm, D_ATT), lambda i, j: (i, 0)),
            pl.BlockSpec((tm, D_SSD), lambda i, j: (i, 0)),
            pl.BlockSpec((None, D_ATT, tn), lambda i, j: (layer, 0, j)),
            pl.BlockSpec((None, D_SSD, tn), lambda i, j: (layer, 1, j)),
            pl.BlockSpec((tm, tn), lambda i, j: (i, j)),
        ],
        out_specs=pl.BlockSpec((tm, tn), lambda i, j: (i, j)),
        out_shape=jax.ShapeDtypeStruct((rows, D_MODEL), f32),
        compiler_params=_cparams(("arbitrary", "arbitrary")),
        name="out_proj",
    )(att, ssd, w, w, x)


FFN_PAD = 2 * SUBLANES


FFN_TF = 512
FFN_Q = FFN_TF // 2


def _ffn_kernel(x_ref, g_ref, w_ref, cw_ref, cb_ref, wd_ref, ci_ref, o_ref, hs_ref,
                xn_ref, hb_ref, ca_ref, act_ref, *, tm, tiles_per_seq, per_tile, rc):
    i = pl.program_id(0)
    j = pl.program_id(1)
    nj = pl.num_programs(1)
    pad = FFN_PAD
    q = FFN_Q

    @pl.when(j == 0)
    def _():
        _rmsnorm_to(x_ref, g_ref, xn_ref, tm)
        o_ref[...] = x_ref[...]

    def conv_act(r, out_rows, c):
        convs = []
        for cols in (slice(2 * c * q, (2 * c + 1) * q), slice((2 * c + 1) * q, (2 * c + 2) * q)):
            y = cb_ref[:, cols] + hb_ref[r - 2:r - 2 + rc, cols] * cw_ref[0:1, cols]
            y = y + hb_ref[r - 1:r - 1 + rc, cols] * cw_ref[1:2, cols]
            y = y + hb_ref[r:r + rc, cols] * cw_ref[2:3, cols]
            convs.append(y)
        act_ref[out_rows, c * q:(c + 1) * q] = (_silu(convs[0]) * convs[1]).astype(bf16)

    if per_tile:
        hb_ref[...] = ci_ref[...]
        h = _dot(xn_ref[...], w_ref[...])
        for b in range(tm // SUBLANES):
            hb_ref[b * pad + SUBLANES:(b + 1) * pad, :] = h[b * SUBLANES:(b + 1) * SUBLANES, :]
        hs_ref[...] = hb_ref[...]
        for b in range(tm // SUBLANES):
            for c in range(2):
                conv_act(b * pad + SUBLANES, slice(b * SUBLANES, (b + 1) * SUBLANES), c)
        o_ref[...] += _dot(act_ref[...], wd_ref[...])
    else:
        first = (i % tiles_per_seq) == 0
        hb_ref[0:pad, :] = jnp.where(first, ci_ref[...], ca_ref[j])

        def up_piece(c):
            cols = slice(2 * c * q, (2 * c + 2) * q)
            hb_ref[pad:pad + tm, cols] = _dot(xn_ref[...], w_ref[:, cols])

        def down_piece(c, out_cols):
            rows = slice(c * q, (c + 1) * q)
            o_ref[:, out_cols] += _dot(act_ref[:, rows], wd_ref[rows, out_cols])

        def conv_rows(c, first_chunk, n):
            for k in range(first_chunk, first_chunk + n):
                conv_act(pad + k * rc, slice(k * rc, (k + 1) * rc), c)

        nch = tm // rc
        lo, hi = slice(0, D_MODEL // 2), slice(D_MODEL // 2, D_MODEL)
        up_piece(0)
        up_piece(1)
        conv_rows(0, 0, nch)
        down_piece(0, lo)
        conv_rows(1, 0, nch // 2)
        down_piece(0, hi)
        conv_rows(1, nch // 2, nch // 2)
        down_piece(1, lo)
        down_piece(1, hi)
        ca_ref[j] = hb_ref[tm:tm + pad, :]
        hs_ref[...] = hb_ref[tm:tm + pad, :]


def _ffn_regroup(a):
    lead = a.shape[:-1]
    n = len(lead)
    a = a.reshape(lead + (2, D_FF // FFN_TF, 2, FFN_Q))
    return jnp.transpose(a, tuple(range(n)) + (n + 1, n + 2, n, n + 3)).reshape(lead + (2 * D_FF,))


def _ffn_ungroup(a):
    lead = a.shape[:-1]
    n = len(lead)
    a = a.reshape(lead + (D_FF // FFN_TF, 2, 2, FFN_Q))
    return jnp.transpose(a, tuple(range(n)) + (n + 2, n, n + 1, n + 3)).reshape(lead + (2 * D_FF,))


def _ffn(x, g, w_up, cw, cb, w_down, cin, layer, *, tm, tiles_per_seq, per_tile):
    rows = x.shape[0]
    tf = FFN_TF
    nj = D_FF // tf
    pad = FFN_PAD
    rc = SUBLANES if per_tile else min(tm, 64)
    hb_rows = (tm // SUBLANES) * pad if per_tile else tm + pad
    st_rows = hb_rows if per_tile else pad
    kern = functools.partial(_ffn_kernel, tm=tm, tiles_per_seq=tiles_per_seq, per_tile=per_tile, rc=rc)
    x_mode = {} if per_tile else dict(pipeline_mode=pl.Buffered(1))
    return pl.pallas_call(
        kern,
        grid=(rows // tm, nj),
        in_specs=[
            pl.BlockSpec((tm, D_MODEL), lambda i, j: (i, 0), **x_mode),
            pl.BlockSpec((None, 1, D_MODEL), lambda i, j: (layer, 0, 0)),
            pl.BlockSpec((None, D_MODEL, 2 * tf), lambda i, j: (layer, 0, j)),
            pl.BlockSpec((None, FFN_CONV, 2 * tf), lambda i, j: (layer, 0, j)),
            pl.BlockSpec((None, 1, 2 * tf), lambda i, j: (layer, 0, j)),
            pl.BlockSpec((None, tf, D_MODEL), lambda i, j: (layer, j, 0)),
            pl.BlockSpec((None, st_rows, 2 * tf), lambda i, j: (i // tiles_per_seq, 0, j)),
        ],
        out_specs=[
            pl.BlockSpec((tm, D_MODEL), lambda i, j: (i, 0)),
            pl.BlockSpec((None, st_rows, 2 * tf), lambda i, j: (i, 0, j)),
        ],
        out_shape=[
            jax.ShapeDtypeStruct((rows, D_MODEL), f32),
            jax.ShapeDtypeStruct((rows // tm, st_rows, 2 * D_FF), f32),
        ],
        scratch_shapes=[
            pltpu.VMEM((tm, D_MODEL), bf16),
            pltpu.VMEM((hb_rows, 2 * tf), f32),
            pltpu.VMEM((nj, pad, 2 * tf), f32),
            pltpu.VMEM((tm, tf), bf16),
        ],
        compiler_params=_cparams(("arbitrary", "arbitrary")),
        name="conv_ffn",
    )(x, g, w_up, cw, cb, w_down, cin)


def _head_norm_rope(x, gain, cos, sin, e2, low_half):
    ssq = _split_dot(x * x, e2, 2)
    xn = x * lax.rsqrt(ssq * (1.0 / HD) + EPS) * gain
    rot = jnp.where(low_half, pltpu.roll(xn, LANES - HD // 2, 1), pltpu.roll(xn, HD // 2, 1))
    return xn * cos + rot * sin


def _attn_block(q_ref, k_ref, ve_ref, q_rows, k_rows, bias_ref, lane_lo):
    qb = q_ref[q_rows, :]
    zero = jnp.zeros_like(qb)
    q2 = jnp.concatenate([jnp.where(lane_lo, qb, zero), jnp.where(lane_lo, zero, qb)], axis=0)
    s = _dot_nt(q2, k_ref[k_rows, :]) + bias_ref[...]
    m = jnp.max(s, axis=1, keepdims=True)
    p = jnp.exp(s - m).astype(bf16)
    r = _dot(p, ve_ref[k_rows, :])
    shape = (CHUNK, LANES)
    o = jnp.where(lane_lo, r[:CHUNK, :LANES], r[CHUNK:, :LANES])
    le = jnp.where(lane_lo, r[:CHUNK, LANES:], r[CHUNK:, LANES:])
    me = jnp.where(lane_lo, jnp.broadcast_to(m[:CHUNK], shape), jnp.broadcast_to(m[CHUNK:], shape))
    return o, me, le


def _attn_kernel(q_ref, k_ref, v_ref, cos_ref, sin_ref, qg_ref, kg_ref, e2_ref, bband_ref, bfirst_ref,
                 att_ref, ko_ref, vo_ref, qf_ref, q4f_ref, k4f_ref, v4f_ref,
                 q1_ref, k1_ref, v1_ref, q4_ref, k4_ref, v4_ref, q16_ref, k16_ref, v16_ref,
                 o1_ref, m1_ref, l1_ref, o4_ref, m4_ref, l4_ref, o16_ref, m16_ref, l16_ref):
    lane_lo = lax.broadcasted_iota(jnp.int32, (CHUNK, LANES), 1) < HD
    rows_a = 256
    lane_a = lax.broadcasted_iota(jnp.int32, (rows_a, LANES), 1)
    low_half_a = (lane_a % HD) < (HD // 2)

    @pl.when((pl.program_id(0) == 0) & (pl.program_id(1) == 0))
    def _():
        ones = jnp.ones((SEQ, LANES), bf16)
        v1_ref[:, LANES:] = ones
        v4_ref[:, LANES:] = ones
        v16_ref[:, LANES:] = ones

    def prep(i, c):
        r = pl.multiple_of(i * rows_a, rows_a)
        sl = pl.ds(r, rows_a)
        cos = cos_ref[sl, :]
        sin = sin_ref[sl, :]
        q = _head_norm_rope(q_ref[sl, :], qg_ref[...], cos, sin, e2_ref[...], low_half_a) * (HD ** -0.5)
        k = _head_norm_rope(k_ref[sl, :], kg_ref[...], cos, sin, e2_ref[...], low_half_a)
        v = v_ref[sl, :]
        qf_ref[sl, :] = q
        ko_ref[sl, :] = k
        vo_ref[sl, :] = v
        q1_ref[sl, :] = q.astype(bf16)
        k1_ref[sl, :] = k.astype(bf16)
        v1_ref[sl, 0:LANES] = v.astype(bf16)
        return c

    lax.fori_loop(0, SEQ // rows_a, prep, 0, unroll=2)

    quarter = SEQ // 4
    streams = ((qf_ref, q4f_ref, q4_ref, q16_ref), (ko_ref, k4f_ref, k4_ref, k16_ref),
               (v_ref, v4f_ref, v4_ref, v16_ref))
    for r in range(4):
        src = pl.ds(r, quarter, stride=4)
        dst = pl.ds(r * quarter, quarter)
        for tok_ref, d4f_ref, d4_ref, _ in streams:
            x = tok_ref[src, :]
            d4f_ref[dst, :] = x
            d4_ref[dst, 0:LANES] = x.astype(bf16)
    for r16 in range(16):
        src = pl.ds((r16 % 4) * quarter + r16 // 4, CHUNK, stride=4)
        dst = pl.ds(r16 * CHUNK, CHUNK)
        for _, d4f_ref, _, d16_ref in streams:
            d16_ref[dst, 0:LANES] = d4f_ref[src, :].astype(bf16)

    def first_block(qd, kd, vd, outs, base, dst):
        rows = pl.ds(base, CHUNK)
        res = _attn_block(qd, kd, vd, rows, rows, bfirst_ref, lane_lo)
        for ref, val in zip(outs, res):
            ref[dst, :] = val

    def band_block(qd, kd, vd, outs, q0, dst):
        k_rows = pl.ds(q0 - CHUNK, 2 * CHUNK)
        res = _attn_block(qd, kd, vd, pl.ds(q0, CHUNK), k_rows, bband_ref, lane_lo)
        for ref, val in zip(outs, res):
            ref[dst, :] = val

    p1 = (q1_ref, k1_ref, v1_ref, (o1_ref, m1_ref, l1_ref))
    p4 = (q4_ref, k4_ref, v4_ref, (o4_ref, m4_ref, l4_ref))
    p16 = (q16_ref, k16_ref, v16_ref, (o16_ref, m16_ref, l16_ref))

    first_block(*p1, 0, pl.ds(0, CHUNK))
    for r in range(4):
        first_block(*p4, r * quarter, pl.ds(r * quarter, CHUNK))

    for c in range(1, SEQ // CHUNK):
        band_block(*p1, c * CHUNK, pl.ds(c * CHUNK, CHUNK))
    for r in range(4):
        for c in range(1, quarter // CHUNK):
            q0 = r * quarter + c * CHUNK
            band_block(*p4, q0, pl.ds(q0, CHUNK))
    for r16 in range(16):
        dst = pl.ds((r16 % 4) * quarter + r16 // 4, CHUNK, stride=4)
        first_block(*p16, r16 * CHUNK, dst)

    for r in range(4):
        def fin(mb, c, r=r):
            sl = pl.ds(pl.multiple_of(r * quarter + mb * CHUNK, CHUNK), CHUNK)
            tok = pl.ds(r + 4 * CHUNK * mb, CHUNK, stride=4)
            m1, m4, m16 = m1_ref[tok, :], m4_ref[sl, :], m16_ref[sl, :]
            mx = jnp.maximum(jnp.maximum(m1, m4), m16)
            a1, a4, a16 = jnp.exp(m1 - mx), jnp.exp(m4 - mx), jnp.exp(m16 - mx)
            num = o1_ref[tok, :] * a1 + o4_ref[sl, :] * a4 + o16_ref[sl, :] * a16
            den = l1_ref[tok, :] * a1 + l4_ref[sl, :] * a4 + l16_ref[sl, :] * a16
            att_ref[tok, :] = num / den
            return c

        lax.fori_loop(0, quarter // CHUNK, fin, 0, unroll=2)


def _attention_prompt(proj3, cos, sin, qg, kg, e2, bias_band, bias_first, layer):
    nhp = D_ATT // LANES
    blk = lambda off: pl.BlockSpec((None, SEQ, LANES), lambda b, h: (b, 0, off + h))
    tab = pl.BlockSpec((SEQ, LANES), lambda b, h: (0, 0))
    gain = pl.BlockSpec((None, 1, LANES), lambda b, h: (layer, 0, 0))
    const = lambda shape: pl.BlockSpec(shape, lambda b, h: (0, 0))
    bscr = lambda w: pltpu.VMEM((SEQ, w), bf16)
    fscr = lambda: pltpu.VMEM((SEQ, LANES), f32)
    return pl.pallas_call(
        _attn_kernel,
        grid=(BATCH, nhp),
        in_specs=[blk(0), blk(nhp), blk(2 * nhp), tab, tab, gain, gain, const((LANES, LANES)),
                  const((2 * CHUNK, 2 * CHUNK)), const((2 * CHUNK, CHUNK))],
        out_specs=[pl.BlockSpec((None, SEQ, LANES), lambda b, h: (b, 0, h))] * 3,
        out_shape=[jax.ShapeDtypeStruct((BATCH, SEQ, D_ATT), f32)] * 3,
        scratch_shapes=([fscr() for _ in range(4)] + [bscr(LANES), bscr(LANES), bscr(2 * LANES)] * 3
                        + [fscr() for _ in range(9)]),
        compiler_params=_cparams(("arbitrary", "arbitrary")),
        name="attn_prompt",
    )(proj3, proj3, proj3, cos, sin, qg, kg, e2, bias_band, bias_first)


def _attn_biases():
    qi = (np.arange(2 * CHUNK) % CHUNK)[:, None]
    ki = np.arange(2 * CHUNK)[None, :]
    band = np.where((ki >= qi) & (ki <= qi + CHUNK), 0.0, -np.inf).astype(np.float32)
    first = np.where(ki[:, :CHUNK] <= qi, 0.0, -np.inf).astype(np.float32)
    return band, first


GRP = 4 * HD
N_CACHED = MAX_WINDOW
KPAD = N_CACHED + LANES


def _attn_sample_kernel(q_ref, k_ref, v_ref, ck_ref, cv_ref, cos_ref, sin_ref,
                        qg_ref, kg_ref, e2_ref, w_ref, att_ref, kn_ref, kc_ref, vc_ref):
    lane = lax.broadcasted_iota(jnp.int32, (S_STEPS, LANES), 1)
    low_half = (lane % HD) < (HD // 2)
    e2 = e2_ref[...]
    qs, ks = [], []
    for t in range(GRP // LANES):
        cs = slice(t * LANES, (t + 1) * LANES)
        cos = cos_ref[:, cs]
        sin = sin_ref[:, cs]
        qs.append(_head_norm_rope(q_ref[:, cs], qg_ref[...], cos, sin, e2, low_half) * (HD ** -0.5))
        ks.append(_head_norm_rope(k_ref[:, cs], kg_ref[...], cos, sin, e2, low_half))
    q = jnp.concatenate(qs, axis=1)
    k = jnp.concatenate(ks, axis=1)
    kn_ref[...] = k

    kc_ref[0:N_CACHED, :] = ck_ref[...].astype(bf16)
    vc_ref[0:N_CACHED, :] = cv_ref[...].astype(bf16)
    kc_ref[N_CACHED:KPAD, :] = jnp.zeros((KPAD - N_CACHED, GRP), bf16)
    vc_ref[N_CACHED:KPAD, :] = jnp.zeros((KPAD - N_CACHED, GRP), bf16)
    kc_ref[N_CACHED:N_CACHED + 2 * S_STEPS, :] = jnp.concatenate(
        [k, jnp.zeros_like(k)], axis=0).astype(bf16)
    vc_ref[N_CACHED:N_CACHED + 2 * S_STEPS, :] = jnp.concatenate(
        [v_ref[...], jnp.zeros_like(k)], axis=0).astype(bf16)

    nq = (GRP // HD) * S_STEPS
    qt = jnp.concatenate([q] * (GRP // HD) + [jnp.zeros((LANES - nq, GRP), f32)], axis=0)
    row_h = lax.broadcasted_iota(jnp.int32, (LANES, GRP), 0) // S_STEPS
    lane_h = lax.broadcasted_iota(jnp.int32, (LANES, GRP), 1) // HD
    same_head = row_h == lane_h
    qt = jnp.where(same_head, qt, 0.0).astype(bf16)
    s = _dot_nt(kc_ref[...], qt)
    w = w_ref[...]
    keep = w > 0.0
    m = jnp.max(jnp.where(keep, s, -jnp.inf), axis=0, keepdims=True)
    e = jnp.where(keep, w * jnp.exp(s - m), 0.0)
    den = jnp.sum(e, axis=0, keepdims=True)
    pt = jnp.transpose(e / den).astype(bf16)
    res = _dot(pt, vc_ref[...])
    res = jnp.where(same_head, res, 0.0)
    out = res[0:S_STEPS]
    for h in range(1, GRP // HD):
        out = out + res[h * S_STEPS:(h + 1) * S_STEPS]
    att_ref[...] = out


def _attention_sample(proj, ck, cv, cos, sin, qg, kg, e2, wmask, layer):
    ng = D_ATT // GRP
    col = lambda off: pl.BlockSpec((S_STEPS, GRP), lambda b, g: (b, off + g))
    cache = pl.BlockSpec((None, None, N_CACHED, GRP), lambda b, g: (layer, b, 0, g))
    tab = pl.BlockSpec((S_STEPS, GRP), lambda b, g: (0, 0))
    gain = pl.BlockSpec((None, 1, LANES), lambda b, g: (layer, 0, 0))
    return pl.pallas_call(
        _attn_sample_kernel,
        grid=(DEC_BATCH, ng),
        in_specs=[col(0), col(ng), col(2 * ng), cache, cache, tab, tab, gain, gain,
                  pl.BlockSpec((LANES, LANES), lambda b, g: (0, 0)),
                  pl.BlockSpec((KPAD, LANES), lambda b, g: (0, 0))],
        out_specs=[pl.BlockSpec((S_STEPS, GRP), lambda b, g: (b, g))] * 2,
        out_shape=[jax.ShapeDtypeStruct((S_ROWS, D_ATT), f32)] * 2,
        scratch_shapes=[pltpu.VMEM((KPAD, GRP), bf16), pltpu.VMEM((KPAD, GRP), bf16)],
        compiler_params=_cparams(("arbitrary", "arbitrary")),
        name="attn_sample",
    )(proj, proj, proj, ck, cv, cos, sin, qg, kg, e2, wmask)


def _sample_key_weights():
    pos = np.full((KPAD,), -1, np.int64)
    pos[:N_CACHED + DEC_SEQ] = np.arange(N_CACHED + DEC_SEQ)
    w = np.zeros((KPAD, LANES), np.float32)
    for t in range(S_STEPS):
        mult = np.zeros((KPAD,), np.float32)
        if t < DEC_SEQ:
            dist = np.where(pos >= 0, MAX_WINDOW + t - pos, -1)
            for d in DILATIONS:
                mult += ((dist >= 0) & (dist % d == 0) & (dist <= WIN_KEYS * d)).astype(np.float32)
        else:
            mult[N_CACHED + t] = 1.0
        for h in range(GRP // HD):
            w[:, h * S_STEPS + t] = mult
    w[0, (GRP // HD) * S_STEPS:] = 1.0
    return w


def _ssd_kernel(z_ref, xbc_ref, dt_ref, ci_ref, h0_ref, cw_ref, cb_ref, dtb_ref, a_ref, dsk_ref,
                ng_ref, tri_ref, exp_ref, y_ref, ho_ref,
                cbuf_ref, xc_ref, st_ref, zb_ref, db_ref, yb_ref,
                *, padded, valid, has_h0):
    c = pl.program_id(1)
    nc = pl.num_programs(1)

    if padded:
        zb_ref[...] = jnp.zeros_like(zb_ref)
        db_ref[...] = jnp.zeros_like(db_ref)
        cbuf_ref[SUBLANES:, :] = jnp.zeros((CHUNK, D_XBC), f32)
        zb_ref[0:S_STEPS, :] = z_ref[...]
        db_ref[0:S_STEPS, :] = dt_ref[...]
        cbuf_ref[SUBLANES:SUBLANES + S_STEPS, :] = xbc_ref[...]
        zsrc, dsrc = zb_ref, db_ref
    else:
        cbuf_ref[SUBLANES:, :] = xbc_ref[...]
        zsrc, dsrc = z_ref, dt_ref

    @pl.when(c == 0)
    def _():
        cbuf_ref[0:SUBLANES, :] = ci_ref[...]
        if has_h0:
            st_ref[...] = jnp.transpose(h0_ref[...])
        else:
            st_ref[...] = jnp.zeros_like(st_ref)

    for t in range(D_XBC // 256):
        cs = slice(t * 256, (t + 1) * 256)
        acc = cb_ref[:, cs] + cbuf_ref[5:5 + CHUNK, cs] * cw_ref[0:1, cs]
        acc = acc + cbuf_ref[6:6 + CHUNK, cs] * cw_ref[1:2, cs]
        acc = acc + cbuf_ref[7:7 + CHUNK, cs] * cw_ref[2:3, cs]
        acc = acc + cbuf_ref[8:8 + CHUNK, cs] * cw_ref[3:4, cs]
        xc_ref[:, cs] = _silu(acc)
    cbuf_ref[0:SUBLANES, :] = cbuf_ref[CHUNK:CHUNK + SUBLANES, :]

    tri = tri_ref[...]
    expand = exp_ref[...]
    dt = _softplus(dsrc[...] + dtb_ref[...])
    a = dt * a_ref[...]
    a_cum = None
    for hi in _split(a, 3):
        t_ = _dot(tri, hi)
        a_cum = t_ if a_cum is None else a_cum + t_
    a_cum_t = jnp.transpose(a_cum)
    dt_e = _split_dot(dt, expand, 2)
    acum_e = _split_dot(a_cum, expand, 3)
    alast_e = acum_e[valid - 1:valid, :]
    row = lax.broadcasted_iota(jnp.int32, (CHUNK, D_SSD), 0)

    xs = xc_ref[:, 0:D_SSD]
    xdt = xs * dt_e
    xdt_b = xdt.astype(bf16)
    xend_b = jnp.where(row < valid, xdt * jnp.exp(alast_e - acum_e), 0.0).astype(bf16)

    ii = lax.broadcasted_iota(jnp.int32, (CHUNK, CHUNK), 0)
    jj = lax.broadcasted_iota(jnp.int32, (CHUNK, CHUNK), 1)
    causal = jj <= ii
    lane_lo = lax.broadcasted_iota(jnp.int32, (CHUNK, LANES), 1) < HD
    hpg = H_S // SSD_GROUPS
    gw = hpg * HD
    for g in range(SSD_GROUPS):
        gs = slice(g * gw, (g + 1) * gw)
        b0 = D_SSD + g * SSD_STATE
        c0 = D_SSD + (SSD_GROUPS + g) * SSD_STATE
        bm = xc_ref[:, b0:b0 + SSD_STATE]
        cm = xc_ref[:, c0:c0 + SSD_STATE].astype(bf16)
        cbm = _dot_nt(cm, bm.astype(bf16))
        bt = jnp.transpose(bm).astype(bf16)
        st_new = _dot(bt, xend_b[:, gs])
        y_off = _dot(cm, st_ref[:, gs].astype(bf16))
        for pr in range(hpg // 2):
            ys = []
            ps = slice(g * gw + pr * LANES, g * gw + (pr + 1) * LANES)
            for hh in range(2):
                h = g * hpg + pr * 2 + hh
                seg = a_cum[:, h:h + 1] - a_cum_t[h:h + 1, :]
                gm = (cbm * jnp.exp(jnp.where(causal, seg, -jnp.inf))).astype(bf16)
                ys.append(_dot(gm, xdt_b[:, ps]))
            yb_ref[:, ps] = jnp.where(lane_lo, ys[0], ys[1])
        yb_ref[:, gs] = yb_ref[:, gs] + y_off * jnp.exp(acum_e[:, gs])
        st_ref[:, gs] = jnp.exp(alast_e[:, gs]) * st_ref[:, gs] + st_new

    y = yb_ref[...] + dsk_ref[...] * xs
    y = y * _silu(zsrc[...])
    ms = jnp.mean(y * y, axis=-1, keepdims=True)
    y = y * lax.rsqrt(ms + EPS) * ng_ref[...]
    if padded:
        y_ref[...] = y[0:S_STEPS]
    else:
        y_ref[...] = y.astype(y_ref.dtype)

    @pl.when(c == nc - 1)
    def _():
        ho_ref[...] = jnp.transpose(st_ref[...])


def _ssd(proj, dt_raw, cinit, h0, cw, cb, dtb, a_neg, dskip, ng, tri, expand, layer, *, sample):
    if sample:
        nb, nc, rows = DEC_BATCH, 1, S_STEPS
        y_dtype = f32
        h0_spec = pl.BlockSpec((None, None, D_SSD, SSD_STATE), lambda b, c: (layer, b, 0, 0))
    else:
        nb, nc, rows = BATCH, SEQ // CHUNK, CHUNK
        y_dtype = bf16
        h0_spec = pl.BlockSpec((None, None, D_SSD, SSD_STATE), lambda b, c: (0, 0, 0, 0))
    rowblk = lambda width, colblk: pl.BlockSpec((rows, width), lambda b, c: (b * nc + c, colblk))
    vec = lambda width: pl.BlockSpec((None, 1, width), lambda b, c: (layer, 0, 0))
    const = lambda shape: pl.BlockSpec(shape, lambda b, c: (0, 0))
    kern = functools.partial(_ssd_kernel, padded=sample, valid=DEC_SEQ if sample else CHUNK,
                             has_h0=sample)
    return pl.pallas_call(
        kern,
        grid=(nb, nc),
        in_specs=[
            rowblk(D_SSD, COL_Z // D_SSD),
            rowblk(D_XBC, COL_XBC // D_XBC),
            rowblk(LANES, 0),
            pl.BlockSpec((None, SUBLANES, D_XBC), lambda b, c: (b, 0, 0)),
            h0_spec,
            pl.BlockSpec((None, SSD_CONV, D_XBC), lambda b, c: (layer, 0, 0)),
            vec(D_XBC), vec(LANES), vec(LANES), vec(D_SSD), vec(D_SSD),
            const((CHUNK, CHUNK)), const((LANES, D_SSD)),
        ],
        out_specs=[rowblk(D_SSD, 0), pl.BlockSpec((None, D_SSD, SSD_STATE), lambda b, c: (b, 0, 0))],
        out_shape=[jax.ShapeDtypeStruct((nb * nc * rows, D_SSD), y_dtype),
                   jax.ShapeDtypeStruct((nb, D_SSD, SSD_STATE), f32)],
        scratch_shapes=[
            pltpu.VMEM((CHUNK + SUBLANES, D_XBC), f32),
            pltpu.VMEM((CHUNK, D_XBC), f32),
            pltpu.VMEM((SSD_STATE, D_SSD), f32),
            pltpu.VMEM((CHUNK, D_SSD), f32),
            pltpu.VMEM((CHUNK, LANES), f32),
            pltpu.VMEM((CHUNK, D_SSD), f32),
        ],
        compiler_params=_cparams(("arbitrary", "arbitrary")),
        name="ssd_sample" if sample else "ssd_prompt",
    )(proj, proj, dt_raw, cinit, h0, cw, cb, dtb, a_neg, dskip, ng, tri, expand)


def _rope_tables(pos, width):
    half = HD // 2
    inv = ROPE_THETA ** (-jnp.arange(half, dtype=f32) / half)
    ang = pos.astype(f32)[:, None] * inv[None, :]
    cos = jnp.cos(ang)
    sin = jnp.sin(ang)
    cos_h = jnp.concatenate([cos, cos], axis=-1)
    sin_h = jnp.concatenate([-sin, sin], axis=-1)
    reps = width // HD
    return jnp.tile(cos_h, (1, reps)), jnp.tile(sin_h, (1, reps))


def kernel(x_prompt, x_sample, cache_win_k, cache_win_v, state_ssd_conv, state_ssd, state_ffn_conv,
           norm1_g, w_in, q_norm_g, k_norm_g, ssd_conv_w, ssd_conv_b, ssd_dt_bias, ssd_a_log,
           ssd_d, ssd_norm_g, w_out, norm2_g, w_up, ffn_conv_w, ffn_conv_b, w_down):
    w_in_b = w_in.astype(bf16)
    w_dt_b = jnp.pad(w_in[:, :, D_PROJ:], ((0, 0), (0, 0), (0, LANES - H_S))).astype(bf16)
    w_out_b = w_out.astype(bf16)
    w_up_b = _ffn_regroup(w_up).astype(bf16)
    w_down_b = w_down.astype(bf16)
    g1 = norm1_g[:, None, :]
    g2 = norm2_g[:, None, :]
    qg = jnp.tile(q_norm_g, (1, LANES // HD))[:, None, :]
    kg = jnp.tile(k_norm_g, (1, LANES // HD))[:, None, :]
    cb_ssd = ssd_conv_b[:, None, :]
    lane_pad = ((0, 0), (0, LANES - H_S))
    dtb = jnp.pad(ssd_dt_bias, lane_pad)[:, None, :]
    a_neg = jnp.pad(-jnp.exp(ssd_a_log.astype(f32)), lane_pad)[:, None, :]
    dskip = jnp.repeat(ssd_d, HD, axis=1)[:, None, :]
    ng = ssd_norm_g[:, None, :]
    cw_ffn = _ffn_regroup(ffn_conv_w)
    cb_ffn = _ffn_regroup(ffn_conv_b)[:, None, :]

    idx = np.arange(LANES)
    e2 = jnp.asarray((idx[:, None] // HD == idx[None, :] // HD).astype(np.float32), dtype=bf16)
    tri = jnp.asarray((idx[None, :] <= idx[:, None]).astype(np.float32), dtype=bf16)
    expand = jnp.asarray((idx[:, None] == np.arange(D_SSD)[None, :] // HD).astype(np.float32), dtype=bf16)
    wmask = jnp.asarray(_sample_key_weights())
    bias_band, bias_first = (jnp.asarray(a) for a in _attn_biases())
    cos_p, sin_p = _rope_tables(jnp.arange(SEQ), LANES)
    cos_s, sin_s = _rope_tables(PAST_LEN + jnp.arange(S_STEPS), GRP)

    xp = x_prompt.reshape(BATCH * SEQ, D_MODEL)
    xs = jnp.pad(x_sample, ((0, 0), (0, S_STEPS - DEC_SEQ), (0, 0))).reshape(S_ROWS, D_MODEL)

    ck = cache_win_k.reshape(DEPTH, DEC_BATCH, MAX_WINDOW, D_ATT)
    cv = cache_win_v.reshape(DEPTH, DEC_BATCH, MAX_WINDOW, D_ATT)
    ssd_ci_p = jnp.zeros((BATCH, SUBLANES, D_XBC), f32)
    ssd_ci_s = jnp.pad(state_ssd_conv, ((0, 0), (0, 0), (SUBLANES - (SSD_CONV - 1), 0), (0, 0)))
    h0_s = state_ssd.reshape(DEPTH, DEC_BATCH, D_SSD, SSD_STATE)
    h0_p = jnp.zeros((1, 1, D_SSD, SSD_STATE), f32)
    ffn_ci_p = jnp.zeros((BATCH, FFN_PAD, 2 * D_FF), f32)
    ffn_ci_s = jnp.pad(_ffn_regroup(state_ffn_conv),
                       ((0, 0), (0, 0), (SUBLANES - (FFN_CONV - 1), S_STEPS), (0, 0))
                       ).reshape(DEPTH, 1, DEC_BATCH * FFN_PAD, 2 * D_FF)

    tm_in, tm_p, tm_ffn = 1024, 512, 1024
    outs = {k: [] for k in ("kp", "vp", "ks", "vs", "cp", "cs", "hp", "hs", "fp", "fs")}
    for i in range(DEPTH):
        proj, dt_raw = _in_proj(xp, g1, w_in_b, w_dt_b, i, tm_in)
        proj3 = proj.reshape(BATCH, SEQ, D_PROJ)
        att, k_rows, v_rows = _attention_prompt(proj3, cos_p, sin_p, qg, kg, e2, bias_band, bias_first, i)
        outs["kp"].append(k_rows.reshape(BATCH, SEQ, H_A, HD))
        outs["vp"].append(v_rows.reshape(BATCH, SEQ, H_A, HD))
        ssd, h_last = _ssd(proj, dt_raw, ssd_ci_p, h0_p, ssd_conv_w, cb_ssd, dtb, a_neg, dskip, ng, tri,
                           expand, i, sample=False)
        x1 = _out_proj(att.reshape(BATCH * SEQ, D_ATT), ssd, w_out_b, xp, i, tm_p)
        xp, hs = _ffn(x1, g2, w_up_b, cw_ffn, cb_ffn, w_down_b, ffn_ci_p, i, tm=tm_ffn,
                      tiles_per_seq=SEQ // tm_ffn, per_tile=False)
        outs["cp"].append(proj3[:, SEQ - (SSD_CONV - 1):, COL_XBC:COL_XBC + D_XBC])
        outs["hp"].append(h_last.reshape(BATCH, H_S, HD, SSD_STATE))
        last = slice(SEQ // tm_ffn - 1, None, SEQ // tm_ffn)
        outs["fp"].append(_ffn_ungroup(hs[last, FFN_PAD - (FFN_CONV - 1):, :]))

        proj_s, dt_s = _in_proj(xs, g1, w_in_b, w_dt_b, i, S_ROWS, tn=D_PROJ // 2)
        att_s, kn_s = _attention_sample(proj_s, ck, cv, cos_s, sin_s, qg, kg, e2, wmask, i)
        ssd_s, h_last_s = _ssd(proj_s, dt_s, ssd_ci_s[i], h0_s, ssd_conv_w, cb_ssd, dtb, a_neg, dskip, ng,
                               tri, expand, i, sample=True)
        x1_s = _out_proj(att_s, ssd_s, w_out_b, xs, i, S_ROWS)
        xs, hs = _ffn(x1_s, g2, w_up_b, cw_ffn, cb_ffn, w_down_b, ffn_ci_s[i], i,
                      tm=S_ROWS, tiles_per_seq=1, per_tile=True)
        p3 = proj_s.reshape(DEC_BATCH, S_STEPS, D_PROJ)
        outs["ks"].append(kn_s.reshape(DEC_BATCH, S_STEPS, H_A, HD)[:, :DEC_SEQ])
        outs["vs"].append(p3[:, :DEC_SEQ, 2 * D_ATT:3 * D_ATT].reshape(DEC_BATCH, DEC_SEQ, H_A, HD))
        outs["cs"].append(p3[:, DEC_SEQ - (SSD_CONV - 1):DEC_SEQ, COL_XBC:COL_XBC + D_XBC])
        outs["hs"].append(h_last_s.reshape(DEC_BATCH, H_S, HD, SSD_STATE))
        lo = SUBLANES + DEC_SEQ - (FFN_CONV - 1)
        hs = hs[0].reshape(DEC_BATCH, FFN_PAD, 2 * D_FF)[:, lo:lo + FFN_CONV - 1, :]
        outs["fs"].append(_ffn_ungroup(hs))

    yp = xp.reshape(BATCH, SEQ, D_MODEL)
    ys = xs.reshape(DEC_BATCH, S_STEPS, D_MODEL)[:, :DEC_SEQ]
    st = lambda k: jnp.stack(outs[k])
    return (yp, ys, st("kp"), st("vp"), st("ks"), st("vs"),
            st("cp"), st("cs"), st("hp"), st("hs"), st("fp"), st("fs"))
```

```python
import functools

import numpy as np
import jax
import jax.numpy as jnp
from jax import lax
from jax.experimental import pallas as pl
from jax.experimental.pallas import tpu as pltpu

f32 = jnp.float32
bf16 = jnp.bfloat16

D_MODEL = 2048
BATCH = 4
SEQ = 2048
DEPTH = 4
DEC_BATCH = 8
DEC_SEQ = 4
PAST_LEN = 16384
HD = 64
D_ATT = 1024
H_A = 16
DILATIONS = (1, 4, 16)
WIN_KEYS = 128
MAX_WINDOW = 2048
ROPE_THETA = 10000.0
D_SSD = 1024
H_S = 16
SSD_STATE = 128
SSD_GROUPS = 4
SSD_CONV = 4
D_XBC = 2048
D_FF = 5632
FFN_CONV = 3
EPS = 1e-6

LANES = 128
SUBLANES = 8
CHUNK = 128
S_STEPS = SUBLANES
S_ROWS = S_STEPS * DEC_BATCH

COL_Z = 3 * D_ATT
COL_XBC = COL_Z + D_SSD
D_PROJ = COL_XBC + D_XBC

VMEM_LIMIT = 56 * 1024 * 1024


def _cparams(sem):
    return pltpu.CompilerParams(dimension_semantics=sem, vmem_limit_bytes=VMEM_LIMIT)


def _dot(a, b):
    return jnp.dot(a, b, preferred_element_type=f32)


def _dot_nt(a, b):
    return lax.dot_general(a, b, (((1,), (1,)), ((), ())), preferred_element_type=f32)


def _split(x, parts):
    out = []
    rem = x
    for p in range(parts):
        hi = rem.astype(bf16)
        out.append(hi)
        if p + 1 < parts:
            rem = rem - hi.astype(f32)
    return out


def _split_dot(x, w, parts):
    acc = None
    for hi in _split(x, parts):
        t = _dot(hi, w)
        acc = t if acc is None else acc + t
    return acc


def _silu(x):
    return x * (1.0 / (1.0 + jnp.exp(-x)))


def _softplus(x):
    return jnp.maximum(x, 0.0) + jnp.log1p(jnp.exp(-jnp.abs(x)))


def _rmsnorm_to(x_ref, g_ref, xn_ref, rows):
    chunk = min(rows, 256)

    def body(i, c):
        r = pl.multiple_of(i * chunk, chunk)
        x = x_ref[pl.ds(r, chunk), :]
        ms = jnp.mean(x * x, axis=-1, keepdims=True)
        xn_ref[pl.ds(r, chunk), :] = (x * lax.rsqrt(ms + EPS) * g_ref[...]).astype(bf16)
        return c

    lax.fori_loop(0, rows // chunk, body, 0)


def _inproj_kernel(x_ref, g_ref, w_ref, wdt_ref, o_ref, dt_ref, xn_ref, *, tm):
    @pl.when(pl.program_id(1) == 0)
    def _():
        _rmsnorm_to(x_ref, g_ref, xn_ref, tm)
        dt_ref[...] = _dot(xn_ref[...], wdt_ref[...])

    o_ref[...] = _dot(xn_ref[...], w_ref[...])


def _in_proj(x, g, w, wdt, layer, tm, tn=1536):
    rows = x.shape[0]
    return pl.pallas_call(
        functools.partial(_inproj_kernel, tm=tm),
        grid=(rows // tm, D_PROJ // tn),
        in_specs=[
            pl.BlockSpec((tm, D_MODEL), lambda i, j: (i, 0)),
            pl.BlockSpec((None, 1, D_MODEL), lambda i, j: (layer, 0, 0)),
            pl.BlockSpec((None, D_MODEL, tn), lambda i, j: (layer, 0, j)),
            pl.BlockSpec((None, D_MODEL, LANES), lambda i, j: (layer, 0, 0)),
        ],
        out_specs=[pl.BlockSpec((tm, tn), lambda i, j: (i, j)),
                   pl.BlockSpec((tm, LANES), lambda i, j: (i, 0))],
        out_shape=[jax.ShapeDtypeStruct((rows, D_PROJ), f32),
                   jax.ShapeDtypeStruct((rows, LANES), f32)],
        scratch_shapes=[pltpu.VMEM((tm, D_MODEL), bf16)],
        compiler_params=_cparams(("arbitrary", "arbitrary")),
        name="in_proj",
    )(x, g, w, wdt)


def _outproj_kernel(a_ref, s_ref, wa_ref, ws_ref, x_ref, o_ref):
    acc = _dot(a_ref[...].astype(bf16), wa_ref[...])
    acc = acc + _dot(s_ref[...].astype(bf16), ws_ref[...])
    o_ref[...] = x_ref[...] + acc


def _out_proj(att, ssd, w, x, layer, tm, tn=D_MODEL):
    rows = x.shape[0]
    return pl.pallas_call(
        _outproj_kernel,
        grid=(rows // tm, D_MODEL // tn),
        in_specs=[
            pl.BlockSpec((tm, D_ATT), lambda i, j: (i, 0)),
            pl.BlockSpec((tm, D_SSD), lambda i, j: (i, 0)),
            pl.BlockSpec((None, D_ATT, tn), lambda i, j: (layer, 0, j)),
            pl.BlockSpec((None, D_SSD, tn), lambda i, j: (layer, 1, j)),
            pl.BlockSpec((tm, tn), lambda i, j: (i, j)),
        ],
        out_specs=pl.BlockSpec((tm, tn), lambda i, j: (i, j)),
        out_shape=jax.ShapeDtypeStruct((rows, D_MODEL), f32),
        compiler_params=_cparams(("arbitrary", "arbitrary")),
        name="out_proj",
    )(att, ssd, w, w, x)


FFN_PAD = 2 * SUBLANES


def _ffn_kernel(x_ref, g_ref, wg_ref, wu_ref, cwg_ref, cwu_ref, cbg_ref, cbu_ref, wd_ref,
                cig_ref, ciu_ref, o_ref, hsg_ref, hsu_ref,
                xn_ref, hbg_ref, hbu_ref, cag_ref, cau_ref, act_ref,
                *, tm, tiles_per_seq, per_tile, rc):
    i = pl.program_id(0)
    j = pl.program_id(1)
    pad = FFN_PAD
    gate = (wg_ref, cig_ref, hbg_ref, cag_ref, hsg_ref, cwg_ref, cbg_ref)
    up = (wu_ref, ciu_ref, hbu_ref, cau_ref, hsu_ref, cwu_ref, cbu_ref)

    @pl.when(j == 0)
    def _():
        _rmsnorm_to(x_ref, g_ref, xn_ref, tm)
        o_ref[...] = x_ref[...]

    def conv_act(r, out_rows):
        convs = []
        for _, _, hb_ref, _, _, cw_ref, cb_ref in (gate, up):
            y = cb_ref[...] + hb_ref[r - 2:r - 2 + rc, :] * cw_ref[0:1, :]
            y = y + hb_ref[r - 1:r - 1 + rc, :] * cw_ref[1:2, :]
            y = y + hb_ref[r:r + rc, :] * cw_ref[2:3, :]
            convs.append(y)
        act_ref[out_rows, :] = (_silu(convs[0]) * convs[1]).astype(bf16)

    if per_tile:
        for w_ref, ci_ref, hb_ref, _, hs_ref, _, _ in (gate, up):
            hb_ref[...] = ci_ref[...]
            h = _dot(xn_ref[...], w_ref[...])
            for b in range(tm // SUBLANES):
                hb_ref[b * pad + SUBLANES:(b + 1) * pad, :] = h[b * SUBLANES:(b + 1) * SUBLANES, :]
            hs_ref[...] = hb_ref[...]
        for b in range(tm // SUBLANES):
            conv_act(b * pad + SUBLANES, slice(b * SUBLANES, (b + 1) * SUBLANES))
        o_ref[...] += _dot(act_ref[...], wd_ref[...])
    else:
        first = (i % tiles_per_seq) == 0
        for _, ci_ref, hb_ref, ca_ref, _, _, _ in (gate, up):
            hb_ref[0:pad, :] = jnp.where(first, ci_ref[...], ca_ref[j])

        half = tm // 2
        halves = (slice(0, half), slice(half, tm))

        def up_piece(rows, branch):
            w_ref, hb_ref = branch[0], branch[2]
            hb_ref[pad + rows.start:pad + rows.stop, :] = _dot(xn_ref[rows, :], w_ref[...])

        def down_piece(rows, cols):
            o_ref[rows, cols] += _dot(act_ref[rows, :], wd_ref[:, cols])

        def conv_rows(r0, n):
            for c in range(n):
                conv_act(pad + r0 + c * rc, slice(r0 + c * rc, r0 + (c + 1) * rc))

        per = half // rc // 2
        lo, hi = slice(0, D_MODEL // 2), slice(D_MODEL // 2, D_MODEL)
        up_piece(halves[0], gate)
        up_piece(halves[0], up)
        up_piece(halves[1], gate)
        conv_rows(0, per)
        up_piece(halves[1], up)
        conv_rows(per * rc, per)
        down_piece(halves[0], lo)
        conv_rows(half, per)
        down_piece(halves[0], hi)
        conv_rows(half + per * rc, per)
        down_piece(halves[1], lo)
        down_piece(halves[1], hi)
        for _, _, hb_ref, ca_ref, hs_ref, _, _ in (gate, up):
            ca_ref[j] = hb_ref[tm:tm + pad, :]
            hs_ref[...] = hb_ref[tm:tm + pad, :]


def _ffn(x, g, w_up, cw, cb, w_down, cin, layer, *, tm, tiles_per_seq, per_tile, tf=512):
    rows = x.shape[0]
    nj = D_FF // tf
    pad = FFN_PAD
    rc = SUBLANES if per_tile else min(tm, 64)
    hb_rows = (tm // SUBLANES) * pad if per_tile else tm + pad
    st_rows = hb_rows if per_tile else pad
    kern = functools.partial(_ffn_kernel, tm=tm, tiles_per_seq=tiles_per_seq, per_tile=per_tile, rc=rc)
    x_mode = {} if per_tile else dict(pipeline_mode=pl.Buffered(1))
    return pl.pallas_call(
        kern,
        grid=(rows // tm, nj),
        in_specs=[
            pl.BlockSpec((tm, D_MODEL), lambda i, j: (i, 0), **x_mode),
            pl.BlockSpec((None, 1, D_MODEL), lambda i, j: (layer, 0, 0)),
            pl.BlockSpec((None, D_MODEL, tf), lambda i, j: (layer, 0, j)),
            pl.BlockSpec((None, D_MODEL, tf), lambda i, j: (layer, 0, nj + j)),
            pl.BlockSpec((None, FFN_CONV, tf), lambda i, j: (layer, 0, j)),
            pl.BlockSpec((None, FFN_CONV, tf), lambda i, j: (layer, 0, nj + j)),
            pl.BlockSpec((None, 1, tf), lambda i, j: (layer, 0, j)),
            pl.BlockSpec((None, 1, tf), lambda i, j: (layer, 0, nj + j)),
            pl.BlockSpec((None, tf, D_MODEL), lambda i, j: (layer, j, 0)),
            pl.BlockSpec((None, st_rows, tf), lambda i, j: (i // tiles_per_seq, 0, j)),
            pl.BlockSpec((None, st_rows, tf), lambda i, j: (i // tiles_per_seq, 0, nj + j)),
        ],
        out_specs=[
            pl.BlockSpec((tm, D_MODEL), lambda i, j: (i, 0)),
            pl.BlockSpec((None, st_rows, tf), lambda i, j: (i, 0, j)),
            pl.BlockSpec((None, st_rows, tf), lambda i, j: (i, 0, j)),
        ],
        out_shape=[
            jax.ShapeDtypeStruct((rows, D_MODEL), f32),
            jax.ShapeDtypeStruct((rows // tm, st_rows, D_FF), f32),
            jax.ShapeDtypeStruct((rows // tm, st_rows, D_FF), f32),
        ],
        scratch_shapes=[
            pltpu.VMEM((tm, D_MODEL), bf16),
            pltpu.VMEM((hb_rows, tf), f32),
            pltpu.VMEM((hb_rows, tf), f32),
            pltpu.VMEM((nj, pad, tf), f32),
            pltpu.VMEM((nj, pad, tf), f32),
            pltpu.VMEM((tm, tf), bf16),
        ],
        compiler_params=_cparams(("arbitrary", "arbitrary")),
        name="conv_ffn",
    )(x, g, w_up, w_up, cw, cw, cb, cb, w_down, cin, cin)


def _head_norm_rope(x, gain, cos, sin, e2, low_half):
    ssq = _split_dot(x * x, e2, 2)
    xn = x * lax.rsqrt(ssq * (1.0 / HD) + EPS) * gain
    rot = jnp.where(low_half, pltpu.roll(xn, LANES - HD // 2, 1), pltpu.roll(xn, HD // 2, 1))
    return xn * cos + rot * sin


def _attn_block(q_ref, k_ref, ve_ref, q_rows, k_rows, bias_ref, lane_lo):
    qb = q_ref[q_rows, :]
    zero = jnp.zeros_like(qb)
    q2 = jnp.concatenate([jnp.where(lane_lo, qb, zero), jnp.where(lane_lo, zero, qb)], axis=0)
    s = _dot_nt(q2, k_ref[k_rows, :]) + bias_ref[...]
    m = jnp.max(s, axis=1, keepdims=True)
    p = jnp.exp(s - m).astype(bf16)
    r = _dot(p, ve_ref[k_rows, :])
    shape = (CHUNK, LANES)
    o = jnp.where(lane_lo, r[:CHUNK, :LANES], r[CHUNK:, :LANES])
    le = jnp.where(lane_lo, r[:CHUNK, LANES:], r[CHUNK:, LANES:])
    me = jnp.where(lane_lo, jnp.broadcast_to(m[:CHUNK], shape), jnp.broadcast_to(m[CHUNK:], shape))
    return o, me, le


def _attn_kernel(q_ref, k_ref, v_ref, cos_ref, sin_ref, qg_ref, kg_ref, e2_ref, bband_ref, bfirst_ref,
                 k_all_ref, v_all_ref,
                 att_ref, ko_ref, vo_ref, qf_ref, q4f_ref, k4f_ref, v4f_ref,
                 q1_ref, k1_ref, v1_ref, q4_ref, k4_ref, v4_ref, q16_ref, k16_ref, v16_ref,
                 o1_ref, m1_ref, l1_ref, o4_ref, m4_ref, l4_ref, o16_ref, m16_ref, l16_ref):
    lane_lo = lax.broadcasted_iota(jnp.int32, (CHUNK, LANES), 1) < HD
    rows_a = 256
    lane_a = lax.broadcasted_iota(jnp.int32, (rows_a, LANES), 1)
    low_half_a = (lane_a % HD) < (HD // 2)

    @pl.when((pl.program_id(0) == 0) & (pl.program_id(1) == 0))
    def _():
        ones = jnp.ones((SEQ, LANES), bf16)
        v1_ref[:, LANES:] = ones
        v4_ref[:, LANES:] = ones
        v16_ref[:, LANES:] = ones

    def prep(i, c):
        r = pl.multiple_of(i * rows_a, rows_a)
        sl = pl.ds(r, rows_a)
        cos = cos_ref[sl, :]
        sin = sin_ref[sl, :]
        q = _head_norm_rope(q_ref[sl, :], qg_ref[...], cos, sin, e2_ref[...], low_half_a) * (HD ** -0.5)
        k = _head_norm_rope(k_ref[sl, :], kg_ref[...], cos, sin, e2_ref[...], low_half_a)
        v = v_ref[sl, :]
        qf_ref[sl, :] = q
        ko_ref[sl, :] = k
        vo_ref[sl, :] = v
        q1_ref[sl, :] = q.astype(bf16)
        k1_ref[sl, :] = k.astype(bf16)
        v1_ref[sl, 0:LANES] = v.astype(bf16)
        return c

    lax.fori_loop(0, SEQ // rows_a, prep, 0, unroll=2)

    quarter = SEQ // 4
    streams = ((qf_ref, q4f_ref, q4_ref, q16_ref), (ko_ref, k4f_ref, k4_ref, k16_ref),
               (v_ref, v4f_ref, v4_ref, v16_ref))
    for r in range(4):
        src = pl.ds(r, quarter, stride=4)
        dst = pl.ds(r * quarter, quarter)
        for tok_ref, d4f_ref, d4_ref, _ in streams:
            x = tok_ref[src, :]
            d4f_ref[dst, :] = x
            d4_ref[dst, 0:LANES] = x.astype(bf16)
    for r16 in range(16):
        src = pl.ds((r16 % 4) * quarter + r16 // 4, CHUNK, stride=4)
        dst = pl.ds(r16 * CHUNK, CHUNK)
        for _, d4f_ref, _, d16_ref in streams:
            d16_ref[dst, 0:LANES] = d4f_ref[src, :].astype(bf16)

    def first_block(qd, kd, vd, outs, base, dst):
        rows = pl.ds(base, CHUNK)
        res = _attn_block(qd, kd, vd, rows, rows, bfirst_ref, lane_lo)
        for ref, val in zip(outs, res):
            ref[dst, :] = val

    def band_block(qd, kd, vd, outs, q0, dst):
        k_rows = pl.ds(q0 - CHUNK, 2 * CHUNK)
        res = _attn_block(qd, kd, vd, pl.ds(q0, CHUNK), k_rows, bband_ref, lane_lo)
        for ref, val in zip(outs, res):
            ref[dst, :] = val

    p1 = (q1_ref, k1_ref, v1_ref, (o1_ref, m1_ref, l1_ref))
    p4 = (q4_ref, k4_ref, v4_ref, (o4_ref, m4_ref, l4_ref))
    p16 = (q16_ref, k16_ref, v16_ref, (o16_ref, m16_ref, l16_ref))

    first_block(*p1, 0, pl.ds(0, CHUNK))
    for r in range(4):
        first_block(*p4, r * quarter, pl.ds(r * quarter, CHUNK))

    for c in range(1, SEQ // CHUNK):
        band_block(*p1, c * CHUNK, pl.ds(c * CHUNK, CHUNK))
    for r in range(4):
        for c in range(1, quarter // CHUNK):
            q0 = r * quarter + c * CHUNK
            band_block(*p4, q0, pl.ds(q0, CHUNK))
    for r16 in range(16):
        dst = pl.ds((r16 % 4) * quarter + r16 // 4, CHUNK, stride=4)
        first_block(*p16, r16 * CHUNK, dst)

    for r in range(4):
        def fin(mb, c, r=r):
            sl = pl.ds(pl.multiple_of(r * quarter + mb * CHUNK, CHUNK), CHUNK)
            tok = pl.ds(r + 4 * CHUNK * mb, CHUNK, stride=4)
            m1, m4, m16 = m1_ref[tok, :], m4_ref[sl, :], m16_ref[sl, :]
            mx = jnp.maximum(jnp.maximum(m1, m4), m16)
            a1, a4, a16 = jnp.exp(m1 - mx), jnp.exp(m4 - mx), jnp.exp(m16 - mx)
            num = o1_ref[tok, :] * a1 + o4_ref[sl, :] * a4 + o16_ref[sl, :] * a16
            den = l1_ref[tok, :] * a1 + l4_ref[sl, :] * a4 + l16_ref[sl, :] * a16
            att_ref[tok, :] = num / den
            return c

        lax.fori_loop(0, quarter // CHUNK, fin, 0, unroll=2)


def _attention_prompt(proj3, cos, sin, qg, kg, e2, bias_band, bias_first, layer, k_all, v_all):
    nhp = D_ATT // LANES
    blk = lambda off: pl.BlockSpec((None, SEQ, LANES), lambda b, h: (b, 0, off + h))
    tab = pl.BlockSpec((SEQ, LANES), lambda b, h: (0, 0))
    gain = pl.BlockSpec((None, 1, LANES), lambda b, h: (layer, 0, 0))
    const = lambda shape: pl.BlockSpec(shape, lambda b, h: (0, 0))
    stacked = pl.BlockSpec((None, None, SEQ, LANES), lambda b, h: (layer, b, 0, h))
    hbm = pl.BlockSpec(memory_space=pl.ANY)
    bscr = lambda w: pltpu.VMEM((SEQ, w), bf16)
    fscr = lambda: pltpu.VMEM((SEQ, LANES), f32)
    n_in = 10
    return pl.pallas_call(
        _attn_kernel,
        grid=(BATCH, nhp),
        in_specs=[blk(0), blk(nhp), blk(2 * nhp), tab, tab, gain, gain, const((LANES, LANES)),
                  const((2 * CHUNK, 2 * CHUNK)), const((2 * CHUNK, CHUNK)), hbm, hbm],
        out_specs=[pl.BlockSpec((None, SEQ, LANES), lambda b, h: (b, 0, h)), stacked, stacked],
        out_shape=[jax.ShapeDtypeStruct((BATCH, SEQ, D_ATT), f32),
                   jax.ShapeDtypeStruct((DEPTH, BATCH, SEQ, D_ATT), f32),
                   jax.ShapeDtypeStruct((DEPTH, BATCH, SEQ, D_ATT), f32)],
        scratch_shapes=([fscr() for _ in range(4)] + [bscr(LANES), bscr(LANES), bscr(2 * LANES)] * 3
                        + [fscr() for _ in range(9)]),
        input_output_aliases={n_in: 1, n_in + 1: 2},
        compiler_params=_cparams(("arbitrary", "arbitrary")),
        name="attn_prompt",
    )(proj3, proj3, proj3, cos, sin, qg, kg, e2, bias_band, bias_first, k_all, v_all)


def _attn_biases():
    qi = (np.arange(2 * CHUNK) % CHUNK)[:, None]
    ki = np.arange(2 * CHUNK)[None, :]
    band = np.where((ki >= qi) & (ki <= qi + CHUNK), 0.0, -np.inf).astype(np.float32)
    first = np.where(ki[:, :CHUNK] <= qi, 0.0, -np.inf).astype(np.float32)
    return band, first


GRP = 4 * HD
N_CACHED = MAX_WINDOW
KPAD = N_CACHED + LANES


def _attn_sample_kernel(q_ref, k_ref, v_ref, ck_ref, cv_ref, cos_ref, sin_ref,
                        qg_ref, kg_ref, e2_ref, w_ref, att_ref, kn_ref, kc_ref, vc_ref):
    lane = lax.broadcasted_iota(jnp.int32, (S_STEPS, LANES), 1)
    low_half = (lane % HD) < (HD // 2)
    e2 = e2_ref[...]
    qs, ks = [], []
    for t in range(GRP // LANES):
        cs = slice(t * LANES, (t + 1) * LANES)
        cos = cos_ref[:, cs]
        sin = sin_ref[:, cs]
        qs.append(_head_norm_rope(q_ref[:, cs], qg_ref[...], cos, sin, e2, low_half) * (HD ** -0.5))
        ks.append(_head_norm_rope(k_ref[:, cs], kg_ref[...], cos, sin, e2, low_half))
    q = jnp.concatenate(qs, axis=1)
    k = jnp.concatenate(ks, axis=1)
    kn_ref[...] = k

    kc_ref[0:N_CACHED, :] = ck_ref[...].astype(bf16)
    vc_ref[0:N_CACHED, :] = cv_ref[...].astype(bf16)
    kc_ref[N_CACHED:KPAD, :] = jnp.zeros((KPAD - N_CACHED, GRP), bf16)
    vc_ref[N_CACHED:KPAD, :] = jnp.zeros((KPAD - N_CACHED, GRP), bf16)
    kc_ref[N_CACHED:N_CACHED + 2 * S_STEPS, :] = jnp.concatenate(
        [k, jnp.zeros_like(k)], axis=0).astype(bf16)
    vc_ref[N_CACHED:N_CACHED + 2 * S_STEPS, :] = jnp.concatenate(
        [v_ref[...], jnp.zeros_like(k)], axis=0).astype(bf16)

    nq = (GRP // HD) * S_STEPS
    qt = jnp.concatenate([q] * (GRP // HD) + [jnp.zeros((LANES - nq, GRP), f32)], axis=0)
    row_h = lax.broadcasted_iota(jnp.int32, (LANES, GRP), 0) // S_STEPS
    lane_h = lax.broadcasted_iota(jnp.int32, (LANES, GRP), 1) // HD
    same_head = row_h == lane_h
    qt = jnp.where(same_head, qt, 0.0).astype(bf16)
    s = _dot_nt(kc_ref[...], qt)
    w = w_ref[...]
    keep = w > 0.0
    m = jnp.max(jnp.where(keep, s, -jnp.inf), axis=0, keepdims=True)
    e = jnp.where(keep, w * jnp.exp(s - m), 0.0)
    den = jnp.sum(e, axis=0, keepdims=True)
    pt = jnp.transpose(e / den).astype(bf16)
    res = _dot(pt, vc_ref[...])
    res = jnp.where(same_head, res, 0.0)
    out = res[0:S_STEPS]
    for h in range(1, GRP // HD):
        out = out + res[h * S_STEPS:(h + 1) * S_STEPS]
    att_ref[...] = out


def _attention_sample(proj, ck, cv, cos, sin, qg, kg, e2, wmask, layer):
    ng = D_ATT // GRP
    col = lambda off: pl.BlockSpec((S_STEPS, GRP), lambda b, g: (b, off + g))
    cache = pl.BlockSpec((None, None, N_CACHED, GRP), lambda b, g: (layer, b, 0, g))
    tab = pl.BlockSpec((S_STEPS, GRP), lambda b, g: (0, 0))
    gain = pl.BlockSpec((None, 1, LANES), lambda b, g: (layer, 0, 0))
    return pl.pallas_call(
        _attn_sample_kernel,
        grid=(DEC_BATCH, ng),
        in_specs=[col(0), col(ng), col(2 * ng), cache, cache, tab, tab, gain, gain,
                  pl.BlockSpec((LANES, LANES), lambda b, g: (0, 0)),
                  pl.BlockSpec((KPAD, LANES), lambda b, g: (0, 0))],
        out_specs=[pl.BlockSpec((S_STEPS, GRP), lambda b, g: (b, g))] * 2,
        out_shape=[jax.ShapeDtypeStruct((S_ROWS, D_ATT), f32)] * 2,
        scratch_shapes=[pltpu.VMEM((KPAD, GRP), bf16), pltpu.VMEM((KPAD, GRP), bf16)],
        compiler_params=_cparams(("arbitrary", "arbitrary")),
        name="attn_sample",
    )(proj, proj, proj, ck, cv, cos, sin, qg, kg, e2, wmask)


def _sample_key_weights():
    pos = np.full((KPAD,), -1, np.int64)
    pos[:N_CACHED + DEC_SEQ] = np.arange(N_CACHED + DEC_SEQ)
    w = np.zeros((KPAD, LANES), np.float32)
    for t in range(S_STEPS):
        mult = np.zeros((KPAD,), np.float32)
        if t < DEC_SEQ:
            dist = np.where(pos >= 0, MAX_WINDOW + t - pos, -1)
            for d in DILATIONS:
                mult += ((dist >= 0) & (dist % d == 0) & (dist <= WIN_KEYS * d)).astype(np.float32)
        else:
            mult[N_CACHED + t] = 1.0
        for h in range(GRP // HD):
            w[:, h * S_STEPS + t] = mult
    w[0, (GRP // HD) * S_STEPS:] = 1.0
    return w


def _ssd_kernel(z_ref, xbc_ref, dt_ref, ci_ref, h0_ref, cw_ref, cb_ref, dtb_ref, a_ref, dsk_ref,
                ng_ref, tri_ref, exp_ref, y_ref, ho_ref,
                cbuf_ref, xc_ref, st_ref, zb_ref, db_ref, yb_ref,
                *, padded, valid, has_h0):
    c = pl.program_id(1)
    nc = pl.num_programs(1)

    if padded:
        zb_ref[...] = jnp.zeros_like(zb_ref)
        db_ref[...] = jnp.zeros_like(db_ref)
        cbuf_ref[SUBLANES:, :] = jnp.zeros((CHUNK, D_XBC), f32)
        zb_ref[0:S_STEPS, :] = z_ref[...]
        db_ref[0:S_STEPS, :] = dt_ref[...]
        cbuf_ref[SUBLANES:SUBLANES + S_STEPS, :] = xbc_ref[...]
        zsrc, dsrc = zb_ref, db_ref
    else:
        cbuf_ref[SUBLANES:, :] = xbc_ref[...]
        zsrc, dsrc = z_ref, dt_ref

    @pl.when(c == 0)
    def _():
        cbuf_ref[0:SUBLANES, :] = ci_ref[...]
        if has_h0:
            st_ref[...] = jnp.transpose(h0_ref[...])
        else:
            st_ref[...] = jnp.zeros_like(st_ref)

    for t in range(D_XBC // 256):
        cs = slice(t * 256, (t + 1) * 256)
        acc = cb_ref[:, cs] + cbuf_ref[5:5 + CHUNK, cs] * cw_ref[0:1, cs]
        acc = acc + cbuf_ref[6:6 + CHUNK, cs] * cw_ref[1:2, cs]
        acc = acc + cbuf_ref[7:7 + CHUNK, cs] * cw_ref[2:3, cs]
        acc = acc + cbuf_ref[8:8 + CHUNK, cs] * cw_ref[3:4, cs]
        xc_ref[:, cs] = _silu(acc)
    cbuf_ref[0:SUBLANES, :] = cbuf_ref[CHUNK:CHUNK + SUBLANES, :]

    tri = tri_ref[...]
    expand = exp_ref[...]
    dt = _softplus(dsrc[...] + dtb_ref[...])
    a = dt * a_ref[...]
    a_cum = None
    for hi in _split(a, 3):
        t_ = _dot(tri, hi)
        a_cum = t_ if a_cum is None else a_cum + t_
    a_cum_t = jnp.transpose(a_cum)
    dt_e = _split_dot(dt, expand, 2)
    acum_e = _split_dot(a_cum, expand, 3)
    alast_e = acum_e[valid - 1:valid, :]
    row = lax.broadcasted_iota(jnp.int32, (CHUNK, D_SSD), 0)

    xs = xc_ref[:, 0:D_SSD]
    xdt = xs * dt_e
    xdt_b = xdt.astype(bf16)
    xend_b = jnp.where(row < valid, xdt * jnp.exp(alast_e - acum_e), 0.0).astype(bf16)

    ii = lax.broadcasted_iota(jnp.int32, (CHUNK, CHUNK), 0)
    jj = lax.broadcasted_iota(jnp.int32, (CHUNK, CHUNK), 1)
    causal = jj <= ii
    lane_lo = lax.broadcasted_iota(jnp.int32, (CHUNK, LANES), 1) < HD
    hpg = H_S // SSD_GROUPS
    gw = hpg * HD
    for g in range(SSD_GROUPS):
        gs = slice(g * gw, (g + 1) * gw)
        b0 = D_SSD + g * SSD_STATE
        c0 = D_SSD + (SSD_GROUPS + g) * SSD_STATE
        bm = xc_ref[:, b0:b0 + SSD_STATE]
        cm = xc_ref[:, c0:c0 + SSD_STATE].astype(bf16)
        cbm = _dot_nt(cm, bm.astype(bf16))
        bt = jnp.transpose(bm).astype(bf16)
        st_new = _dot(bt, xend_b[:, gs])
        y_off = _dot(cm, st_ref[:, gs].astype(bf16))
        for pr in range(hpg // 2):
            ys = []
            ps = slice(g * gw + pr * LANES, g * gw + (pr + 1) * LANES)
            for hh in range(2):
                h = g * hpg + pr * 2 + hh
                seg = a_cum[:, h:h + 1] - a_cum_t[h:h + 1, :]
                gm = (cbm * jnp.exp(jnp.where(causal, seg, -jnp.inf))).astype(bf16)
                ys.append(_dot(gm, xdt_b[:, ps]))
            yb_ref[:, ps] = jnp.where(lane_lo, ys[0], ys[1])
        yb_ref[:, gs] = yb_ref[:, gs] + y_off * jnp.exp(acum_e[:, gs])
        st_ref[:, gs] = jnp.exp(alast_e[:, gs]) * st_ref[:, gs] + st_new

    y = yb_ref[...] + dsk_ref[...] * xs
    y = y * _silu(zsrc[...])
    ms = jnp.mean(y * y, axis=-1, keepdims=True)
    y = y * lax.rsqrt(ms + EPS) * ng_ref[...]
    if padded:
        y_ref[...] = y[0:S_STEPS]
    else:
        y_ref[...] = y.astype(y_ref.dtype)

    @pl.when(c == nc - 1)
    def _():
        ho_ref[...] = jnp.transpose(st_ref[...])


def _ssd(proj, dt_raw, cinit, h0, cw, cb, dtb, a_neg, dskip, ng, tri, expand, layer, *, sample):
    if sample:
        nb, nc, rows = DEC_BATCH, 1, S_STEPS
        y_dtype = f32
        h0_spec = pl.BlockSpec((None, None, D_SSD, SSD_STATE), lambda b, c: (layer, b, 0, 0))
    else:
        nb, nc, rows = BATCH, SEQ // CHUNK, CHUNK
        y_dtype = bf16
        h0_spec = pl.BlockSpec((None, None, D_SSD, SSD_STATE), lambda b, c: (0, 0, 0, 0))
    rowblk = lambda width, colblk: pl.BlockSpec((rows, width), lambda b, c: (b * nc + c, colblk))
    vec = lambda width: pl.BlockSpec((None, 1, width), lambda b, c: (layer, 0, 0))
    const = lambda shape: pl.BlockSpec(shape, lambda b, c: (0, 0))
    kern = functools.partial(_ssd_kernel, padded=sample, valid=DEC_SEQ if sample else CHUNK,
                             has_h0=sample)
    return pl.pallas_call(
        kern,
        grid=(nb, nc),
        in_specs=[
            rowblk(D_SSD, COL_Z // D_SSD),
            rowblk(D_XBC, COL_XBC // D_XBC),
            rowblk(LANES, 0),
            pl.BlockSpec((None, SUBLANES, D_XBC), lambda b, c: (b, 0, 0)),
            h0_spec,
            pl.BlockSpec((None, SSD_CONV, D_XBC), lambda b, c: (layer, 0, 0)),
            vec(D_XBC), vec(LANES), vec(LANES), vec(D_SSD), vec(D_SSD),
            const((CHUNK, CHUNK)), const((LANES, D_SSD)),
        ],
        out_specs=[rowblk(D_SSD, 0), pl.BlockSpec((None, D_SSD, SSD_STATE), lambda b, c: (b, 0, 0))],
        out_shape=[jax.ShapeDtypeStruct((nb * nc * rows, D_SSD), y_dtype),
                   jax.ShapeDtypeStruct((nb, D_SSD, SSD_STATE), f32)],
        scratch_shapes=[
            pltpu.VMEM((CHUNK + SUBLANES, D_XBC), f32),
            pltpu.VMEM((CHUNK, D_XBC), f32),
            pltpu.VMEM((SSD_STATE, D_SSD), f32),
            pltpu.VMEM((CHUNK, D_SSD), f32),
            pltpu.VMEM((CHUNK, LANES), f32),
            pltpu.VMEM((CHUNK, D_SSD), f32),
        ],
        compiler_params=_cparams(("arbitrary", "arbitrary")),
        name="ssd_sample" if sample else "ssd_prompt",
    )(proj, proj, dt_raw, cinit, h0, cw, cb, dtb, a_neg, dskip, ng, tri, expand)


def _rope_tables(pos, width):
    half = HD // 2
    inv = ROPE_THETA ** (-jnp.arange(half, dtype=f32) / half)
    ang = pos.astype(f32)[:, None] * inv[None, :]
    cos = jnp.cos(ang)
    sin = jnp.sin(ang)
    cos_h = jnp.concatenate([cos, cos], axis=-1)
    sin_h = jnp.concatenate([-sin, sin], axis=-1)
    reps = width // HD
    return jnp.tile(cos_h, (1, reps)), jnp.tile(sin_h, (1, reps))


def kernel(x_prompt, x_sample, cache_win_k, cache_win_v, state_ssd_conv, state_ssd, state_ffn_conv,
           norm1_g, w_in, q_norm_g, k_norm_g, ssd_conv_w, ssd_conv_b, ssd_dt_bias, ssd_a_log,
           ssd_d, ssd_norm_g, w_out, norm2_g, w_up, ffn_conv_w, ffn_conv_b, w_down):
    w_in_b = w_in.astype(bf16)
    w_dt_b = jnp.pad(w_in[:, :, D_PROJ:], ((0, 0), (0, 0), (0, LANES - H_S))).astype(bf16)
    w_out_b = w_out.astype(bf16)
    w_up_b = w_up.astype(bf16)
    w_down_b = w_down.astype(bf16)
    g1 = norm1_g[:, None, :]
    g2 = norm2_g[:, None, :]
    qg = jnp.tile(q_norm_g, (1, LANES // HD))[:, None, :]
    kg = jnp.tile(k_norm_g, (1, LANES // HD))[:, None, :]
    cb_ssd = ssd_conv_b[:, None, :]
    lane_pad = ((0, 0), (0, LANES - H_S))
    dtb = jnp.pad(ssd_dt_bias, lane_pad)[:, None, :]
    a_neg = jnp.pad(-jnp.exp(ssd_a_log.astype(f32)), lane_pad)[:, None, :]
    dskip = jnp.repeat(ssd_d, HD, axis=1)[:, None, :]
    ng = ssd_norm_g[:, None, :]
    cb_ffn = ffn_conv_b[:, None, :]

    idx = np.arange(LANES)
    e2 = jnp.asarray((idx[:, None] // HD == idx[None, :] // HD).astype(np.float32), dtype=bf16)
    tri = jnp.asarray((idx[None, :] <= idx[:, None]).astype(np.float32), dtype=bf16)
    expand = jnp.asarray((idx[:, None] == np.arange(D_SSD)[None, :] // HD).astype(np.float32), dtype=bf16)
    wmask = jnp.asarray(_sample_key_weights())
    bias_band, bias_first = (jnp.asarray(a) for a in _attn_biases())
    cos_p, sin_p = _rope_tables(jnp.arange(SEQ), LANES)
    cos_s, sin_s = _rope_tables(PAST_LEN + jnp.arange(S_STEPS), GRP)

    xp = x_prompt.reshape(BATCH * SEQ, D_MODEL)
    xs = jnp.pad(x_sample, ((0, 0), (0, S_STEPS - DEC_SEQ), (0, 0))).reshape(S_ROWS, D_MODEL)

    ck = cache_win_k.reshape(DEPTH, DEC_BATCH, MAX_WINDOW, D_ATT)
    cv = cache_win_v.reshape(DEPTH, DEC_BATCH, MAX_WINDOW, D_ATT)
    ssd_ci_p = jnp.zeros((BATCH, SUBLANES, D_XBC), f32)
    ssd_ci_s = jnp.pad(state_ssd_conv, ((0, 0), (0, 0), (SUBLANES - (SSD_CONV - 1), 0), (0, 0)))
    h0_s = state_ssd.reshape(DEPTH, DEC_BATCH, D_SSD, SSD_STATE)
    h0_p = jnp.zeros((1, 1, D_SSD, SSD_STATE), f32)
    ffn_ci_p = jnp.zeros((BATCH, FFN_PAD, 2 * D_FF), f32)
    ffn_ci_s = jnp.pad(state_ffn_conv, ((0, 0), (0, 0), (SUBLANES - (FFN_CONV - 1), S_STEPS), (0, 0))
                       ).reshape(DEPTH, 1, DEC_BATCH * FFN_PAD, 2 * D_FF)

    tm_in, tm_p, tm_ffn = 1024, 512, 1024
    k_all = jnp.zeros((DEPTH, BATCH, SEQ, D_ATT), f32)
    v_all = jnp.zeros((DEPTH, BATCH, SEQ, D_ATT), f32)
    outs = {k: [] for k in ("ks", "vs", "cp", "cs", "hp", "hs", "fp", "fs")}
    for i in range(DEPTH):
        proj, dt_raw = _in_proj(xp, g1, w_in_b, w_dt_b, i, tm_in)
        proj3 = proj.reshape(BATCH, SEQ, D_PROJ)
        att, k_all, v_all = _attention_prompt(proj3, cos_p, sin_p, qg, kg, e2, bias_band, bias_first, i,
                                              k_all, v_all)
        ssd, h_last = _ssd(proj, dt_raw, ssd_ci_p, h0_p, ssd_conv_w, cb_ssd, dtb, a_neg, dskip, ng, tri,
                           expand, i, sample=False)
        x1 = _out_proj(att.reshape(BATCH * SEQ, D_ATT), ssd, w_out_b, xp, i, tm_p)
        xp, hs_g, hs_u = _ffn(x1, g2, w_up_b, ffn_conv_w, cb_ffn, w_down_b, ffn_ci_p, i, tm=tm_ffn,
                              tiles_per_seq=SEQ // tm_ffn, per_tile=False)
        outs["cp"].append(proj3[:, SEQ - (SSD_CONV - 1):, COL_XBC:COL_XBC + D_XBC])
        outs["hp"].append(h_last.reshape(BATCH, H_S, HD, SSD_STATE))
        last = slice(SEQ // tm_ffn - 1, None, SEQ // tm_ffn)
        hs = jnp.concatenate([hs_g[last], hs_u[last]], axis=-1)
        outs["fp"].append(hs[:, FFN_PAD - (FFN_CONV - 1):, :])

        proj_s, dt_s = _in_proj(xs, g1, w_in_b, w_dt_b, i, S_ROWS, tn=D_PROJ // 2)
        att_s, kn_s = _attention_sample(proj_s, ck, cv, cos_s, sin_s, qg, kg, e2, wmask, i)
        ssd_s, h_last_s = _ssd(proj_s, dt_s, ssd_ci_s[i], h0_s, ssd_conv_w, cb_ssd, dtb, a_neg, dskip, ng,
                               tri, expand, i, sample=True)
        x1_s = _out_proj(att_s, ssd_s, w_out_b, xs, i, S_ROWS)
        xs, hs_g, hs_u = _ffn(x1_s, g2, w_up_b, ffn_conv_w, cb_ffn, w_down_b, ffn_ci_s[i], i,
                              tm=S_ROWS, tiles_per_seq=1, per_tile=True)
        p3 = proj_s.reshape(DEC_BATCH, S_STEPS, D_PROJ)
        outs["ks"].append(kn_s.reshape(DEC_BATCH, S_STEPS, H_A, HD)[:, :DEC_SEQ])
        outs["vs"].append(p3[:, :DEC_SEQ, 2 * D_ATT:3 * D_ATT].reshape(DEC_BATCH, DEC_SEQ, H_A, HD))
        outs["cs"].append(p3[:, DEC_SEQ - (SSD_CONV - 1):DEC_SEQ, COL_XBC:COL_XBC + D_XBC])
        outs["hs"].append(h_last_s.reshape(DEC_BATCH, H_S, HD, SSD_STATE))
        hs = jnp.concatenate([hs_g[0], hs_u[0]], axis=-1).reshape(DEC_BATCH, FFN_PAD, 2 * D_FF)
        lo = SUBLANES + DEC_SEQ - (FFN_CONV - 1)
        outs["fs"].append(hs[:, lo:lo + FFN_CONV - 1, :])

    yp = xp.reshape(BATCH, SEQ, D_MODEL)
    ys = xs.reshape(DEC_BATCH, S_STEPS, D_MODEL)[:, :DEC_SEQ]
    st = lambda k: jnp.stack(outs[k])
    win_shape = (DEPTH, BATCH, SEQ, H_A, HD)
    return (yp, ys, k_all.reshape(win_shape), v_all.reshape(win_shape), st("ks"), st("vs"),
            st("cp"), st("cs"), st("hp"), st("hs"), st("fp"), st("fs"))
```

```python
import functools

import numpy as np
import jax
import jax.numpy as jnp
from jax import lax
from jax.experimental import pallas as pl
from jax.experimental.pallas import tpu as pltpu

f32 = jnp.float32
bf16 = jnp.bfloat16

D_MODEL = 2048
BATCH = 4
SEQ = 2048
DEPTH = 4
DEC_BATCH = 8
DEC_SEQ = 4
PAST_LEN = 16384
HD = 64
D_ATT = 1024
H_A = 16
DILATIONS = (1, 4, 16)
WIN_KEYS = 128
MAX_WINDOW = 2048
ROPE_THETA = 10000.0
D_SSD = 1024
H_S = 16
SSD_STATE = 128
SSD_GROUPS = 4
SSD_CONV = 4
D_XBC = 2048
D_FF = 5632
FFN_CONV = 3
EPS = 1e-6

LANES = 128
SUBLANES = 8
CHUNK = 128
S_STEPS = SUBLANES
S_ROWS = S_STEPS * DEC_BATCH

COL_Z = 3 * D_ATT
COL_XBC = COL_Z + D_SSD
D_PROJ = COL_XBC + D_XBC

VMEM_LIMIT = 56 * 1024 * 1024


def _cparams(sem):
    return pltpu.CompilerParams(dimension_semantics=sem, vmem_limit_bytes=VMEM_LIMIT)


def _dot(a, b):
    return jnp.dot(a, b, preferred_element_type=f32)


def _dot_nt(a, b):
    return lax.dot_general(a, b, (((1,), (1,)), ((), ())), preferred_element_type=f32)


def _split(x, parts):
    out = []
    rem = x
    for p in range(parts):
        hi = rem.astype(bf16)
        out.append(hi)
        if p + 1 < parts:
            rem = rem - hi.astype(f32)
    return out


def _split_dot(x, w, parts):
    acc = None
    for hi in _split(x, parts):
        t = _dot(hi, w)
        acc = t if acc is None else acc + t
    return acc


def _silu(x):
    return x * (1.0 / (1.0 + jnp.exp(-x)))


def _softplus(x):
    return jnp.maximum(x, 0.0) + jnp.log1p(jnp.exp(-jnp.abs(x)))


def _rmsnorm_to(x_ref, g_ref, xn_ref, rows):
    chunk = min(rows, 256)

    def body(i, c):
        r = pl.multiple_of(i * chunk, chunk)
        x = x_ref[pl.ds(r, chunk), :]
        ms = jnp.mean(x * x, axis=-1, keepdims=True)
        xn_ref[pl.ds(r, chunk), :] = (x * lax.rsqrt(ms + EPS) * g_ref[...]).astype(bf16)
        return c

    lax.fori_loop(0, rows // chunk, body, 0)


def _inproj_kernel(x_ref, g_ref, w_ref, wdt_ref, o_ref, dt_ref, xn_ref, *, tm):
    @pl.when(pl.program_id(1) == 0)
    def _():
        _rmsnorm_to(x_ref, g_ref, xn_ref, tm)
        dt_ref[...] = _dot(xn_ref[...], wdt_ref[...])

    o_ref[...] = _dot(xn_ref[...], w_ref[...])


def _in_proj(x, g, w, wdt, layer, tm, tn=1536):
    rows = x.shape[0]
    return pl.pallas_call(
        functools.partial(_inproj_kernel, tm=tm),
        grid=(rows // tm, D_PROJ // tn),
        in_specs=[
            pl.BlockSpec((tm, D_MODEL), lambda i, j: (i, 0)),
            pl.BlockSpec((None, 1, D_MODEL), lambda i, j: (layer, 0, 0)),
            pl.BlockSpec((None, D_MODEL, tn), lambda i, j: (layer, 0, j)),
            pl.BlockSpec((None, D_MODEL, LANES), lambda i, j: (layer, 0, 0)),
        ],
        out_specs=[pl.BlockSpec((tm, tn), lambda i, j: (i, j)),
                   pl.BlockSpec((tm, LANES), lambda i, j: (i, 0))],
        out_shape=[jax.ShapeDtypeStruct((rows, D_PROJ), f32),
                   jax.ShapeDtypeStruct((rows, LANES), f32)],
        scratch_shapes=[pltpu.VMEM((tm, D_MODEL), bf16)],
        compiler_params=_cparams(("arbitrary", "arbitrary")),
        name="in_proj",
    )(x, g, w, wdt)


def _outproj_kernel(a_ref, s_ref, wa_ref, ws_ref, x_ref, o_ref):
    acc = _dot(a_ref[...].astype(bf16), wa_ref[...])
    acc = acc + _dot(s_ref[...].astype(bf16), ws_ref[...])
    o_ref[...] = x_ref[...] + acc


def _out_proj(att, ssd, w, x, layer, tm, tn=D_MODEL):
    rows = x.shape[0]
    return pl.pallas_call(
        _outproj_kernel,
        grid=(rows // tm, D_MODEL // tn),
        in_specs=[
            pl.BlockSpec((tm, D_ATT), lambda i, j: (i, 0)),
            pl.BlockSpec((tm, D_SSD), lambda i, j: (i, 0)),
            pl.BlockSpec((None, D_ATT, tn), lambda i, j: (layer, 0, j)),
            pl.BlockSpec((None, D_SSD, tn), lambda i, j: (layer, 1, j)),
            pl.BlockSpec((tm, tn), lambda i, j: (i, j)),
        ],
        out_specs=pl.BlockSpec((tm, tn), lambda i, j: (i, j)),
        out_shape=jax.ShapeDtypeStruct((rows, D_MODEL), f32),
        compiler_params=_cparams(("arbitrary", "arbitrary")),
        name="out_proj",
    )(att, ssd, w, w, x)


FFN_PAD = 2 * SUBLANES


def _ffn_kernel(x_ref, g_ref, wg_ref, wu_ref, cwg_ref, cwu_ref, cbg_ref, cbu_ref, wd_ref,
                cig_ref, ciu_ref, o_ref, hsg_ref, hsu_ref,
                xn_ref, hbg_ref, hbu_ref, cag_ref, cau_ref, act_ref,
                *, tm, tiles_per_seq, per_tile, rc):
    i = pl.program_id(0)
    j = pl.program_id(1)
    pad = FFN_PAD
    gate = (wg_ref, cig_ref, hbg_ref, cag_ref, hsg_ref, cwg_ref, cbg_ref)
    up = (wu_ref, ciu_ref, hbu_ref, cau_ref, hsu_ref, cwu_ref, cbu_ref)

    @pl.when(j == 0)
    def _():
        _rmsnorm_to(x_ref, g_ref, xn_ref, tm)
        o_ref[...] = x_ref[...]

    def conv_act(r, out_rows):
        convs = []
        for _, _, hb_ref, _, _, cw_ref, cb_ref in (gate, up):
            y = cb_ref[...] + hb_ref[r - 2:r - 2 + rc, :] * cw_ref[0:1, :]
            y = y + hb_ref[r - 1:r - 1 + rc, :] * cw_ref[1:2, :]
            y = y + hb_ref[r:r + rc, :] * cw_ref[2:3, :]
            convs.append(y)
        act_ref[out_rows, :] = (_silu(convs[0]) * convs[1]).astype(bf16)

    if per_tile:
        for w_ref, ci_ref, hb_ref, _, hs_ref, _, _ in (gate, up):
            hb_ref[...] = ci_ref[...]
            h = _dot(xn_ref[...], w_ref[...])
            for b in range(tm // SUBLANES):
                hb_ref[b * pad + SUBLANES:(b + 1) * pad, :] = h[b * SUBLANES:(b + 1) * SUBLANES, :]
            hs_ref[...] = hb_ref[...]
        for b in range(tm // SUBLANES):
            conv_act(b * pad + SUBLANES, slice(b * SUBLANES, (b + 1) * SUBLANES))
        o_ref[...] += _dot(act_ref[...], wd_ref[...])
    else:
        first = (i % tiles_per_seq) == 0
        for _, ci_ref, hb_ref, ca_ref, _, _, _ in (gate, up):
            hb_ref[0:pad, :] = jnp.where(first, ci_ref[...], ca_ref[j])

        half = tm // 2
        halves = (slice(0, half), slice(half, tm))

        def up_piece(rows, branch):
            w_ref, hb_ref = branch[0], branch[2]
            hb_ref[pad + rows.start:pad + rows.stop, :] = _dot(xn_ref[rows, :], w_ref[...])

        def down_piece(rows, cols):
            o_ref[rows, cols] += _dot(act_ref[rows, :], wd_ref[:, cols])

        def conv_rows(r0, n):
            for c in range(n):
                conv_act(pad + r0 + c * rc, slice(r0 + c * rc, r0 + (c + 1) * rc))

        per = half // rc // 2
        lo, hi = slice(0, D_MODEL // 2), slice(D_MODEL // 2, D_MODEL)
        up_piece(halves[0], gate)
        up_piece(halves[0], up)
        up_piece(halves[1], gate)
        conv_rows(0, per)
        up_piece(halves[1], up)
        conv_rows(per * rc, per)
        down_piece(halves[0], lo)
        conv_rows(half, per)
        down_piece(halves[0], hi)
        conv_rows(half + per * rc, per)
        down_piece(halves[1], lo)
        down_piece(halves[1], hi)
        for _, _, hb_ref, ca_ref, hs_ref, _, _ in (gate, up):
            ca_ref[j] = hb_ref[tm:tm + pad, :]
            hs_ref[...] = hb_ref[tm:tm + pad, :]


def _ffn(x, g, w_up, cw, cb, w_down, cin, layer, *, tm, tiles_per_seq, per_tile, tf=512):
    rows = x.shape[0]
    nj = D_FF // tf
    pad = FFN_PAD
    rc = SUBLANES if per_tile else min(tm, 64)
    hb_rows = (tm // SUBLANES) * pad if per_tile else tm + pad
    st_rows = hb_rows if per_tile else pad
    kern = functools.partial(_ffn_kernel, tm=tm, tiles_per_seq=tiles_per_seq, per_tile=per_tile, rc=rc)
    x_mode = {} if per_tile else dict(pipeline_mode=pl.Buffered(1))
    return pl.pallas_call(
        kern,
        grid=(rows // tm, nj),
        in_specs=[
            pl.BlockSpec((tm, D_MODEL), lambda i, j: (i, 0), **x_mode),
            pl.BlockSpec((None, 1, D_MODEL), lambda i, j: (layer, 0, 0)),
            pl.BlockSpec((None, D_MODEL, tf), lambda i, j: (layer, 0, j)),
            pl.BlockSpec((None, D_MODEL, tf), lambda i, j: (layer, 0, nj + j)),
            pl.BlockSpec((None, FFN_CONV, tf), lambda i, j: (layer, 0, j)),
            pl.BlockSpec((None, FFN_CONV, tf), lambda i, j: (layer, 0, nj + j)),
            pl.BlockSpec((None, 1, tf), lambda i, j: (layer, 0, j)),
            pl.BlockSpec((None, 1, tf), lambda i, j: (layer, 0, nj + j)),
            pl.BlockSpec((None, tf, D_MODEL), lambda i, j: (layer, j, 0)),
            pl.BlockSpec((None, st_rows, tf), lambda i, j: (i // tiles_per_seq, 0, j)),
            pl.BlockSpec((None, st_rows, tf), lambda i, j: (i // tiles_per_seq, 0, nj + j)),
        ],
        out_specs=[
            pl.BlockSpec((tm, D_MODEL), lambda i, j: (i, 0)),
            pl.BlockSpec((None, st_rows, tf), lambda i, j: (i, 0, j)),
            pl.BlockSpec((None, st_rows, tf), lambda i, j: (i, 0, j)),
        ],
        out_shape=[
            jax.ShapeDtypeStruct((rows, D_MODEL), f32),
            jax.ShapeDtypeStruct((rows // tm, st_rows, D_FF), f32),
            jax.ShapeDtypeStruct((rows // tm, st_rows, D_FF), f32),
        ],
        scratch_shapes=[
            pltpu.VMEM((tm, D_MODEL), bf16),
            pltpu.VMEM((hb_rows, tf), f32),
            pltpu.VMEM((hb_rows, tf), f32),
            pltpu.VMEM((nj, pad, tf), f32),
            pltpu.VMEM((nj, pad, tf), f32),
            pltpu.VMEM((tm, tf), bf16),
        ],
        compiler_params=_cparams(("arbitrary", "arbitrary")),
        name="conv_ffn",
    )(x, g, w_up, w_up, cw, cw, cb, cb, w_down, cin, cin)


def _head_norm_rope(x, gain, cos, sin, e2, low_half):
    ssq = _split_dot(x * x, e2, 2)
    xn = x * lax.rsqrt(ssq * (1.0 / HD) + EPS) * gain
    rot = jnp.where(low_half, pltpu.roll(xn, LANES - HD // 2, 1), pltpu.roll(xn, HD // 2, 1))
    return xn * cos + rot * sin


def _attn_block(q_ref, k_ref, ve_ref, q_rows, k_rows, bias_ref, lane_lo):
    qb = q_ref[q_rows, :]
    zero = jnp.zeros_like(qb)
    q2 = jnp.concatenate([jnp.where(lane_lo, qb, zero), jnp.where(lane_lo, zero, qb)], axis=0)
    s = _dot_nt(q2, k_ref[k_rows, :]) + bias_ref[...]
    m = jnp.max(s, axis=1, keepdims=True)
    p = jnp.exp(s - m).astype(bf16)
    r = _dot(p, ve_ref[k_rows, :])
    shape = (CHUNK, LANES)
    o = jnp.where(lane_lo, r[:CHUNK, :LANES], r[CHUNK:, :LANES])
    le = jnp.where(lane_lo, r[:CHUNK, LANES:], r[CHUNK:, LANES:])
    me = jnp.where(lane_lo, jnp.broadcast_to(m[:CHUNK], shape), jnp.broadcast_to(m[CHUNK:], shape))
    return o, me, le


def _attn_kernel(q_ref, k_ref, v_ref, cos_ref, sin_ref, qg_ref, kg_ref, e2_ref, bband_ref, bfirst_ref,
                 k_all_ref, v_all_ref,
                 att_ref, ko_ref, vo_ref, qf_ref, q4f_ref, k4f_ref, v4f_ref,
                 q1_ref, k1_ref, v1_ref, q4_ref, k4_ref, v4_ref, q16_ref, k16_ref, v16_ref,
                 o1_ref, m1_ref, l1_ref, o4_ref, m4_ref, l4_ref, o16_ref, m16_ref, l16_ref):
    lane_lo = lax.broadcasted_iota(jnp.int32, (CHUNK, LANES), 1) < HD
    rows_a = 256
    lane_a = lax.broadcasted_iota(jnp.int32, (rows_a, LANES), 1)
    low_half_a = (lane_a % HD) < (HD // 2)

    @pl.when((pl.program_id(0) == 0) & (pl.program_id(1) == 0))
    def _():
        ones = jnp.ones((SEQ, LANES), bf16)
        v1_ref[:, LANES:] = ones
        v4_ref[:, LANES:] = ones
        v16_ref[:, LANES:] = ones

    def prep(i, c):
        r = pl.multiple_of(i * rows_a, rows_a)
        sl = pl.ds(r, rows_a)
        cos = cos_ref[sl, :]
        sin = sin_ref[sl, :]
        q = _head_norm_rope(q_ref[sl, :], qg_ref[...], cos, sin, e2_ref[...], low_half_a) * (HD ** -0.5)
        k = _head_norm_rope(k_ref[sl, :], kg_ref[...], cos, sin, e2_ref[...], low_half_a)
        v = v_ref[sl, :]
        qf_ref[sl, :] = q
        ko_ref[sl, :] = k
        vo_ref[sl, :] = v
        q1_ref[sl, :] = q.astype(bf16)
        k1_ref[sl, :] = k.astype(bf16)
        v1_ref[sl, 0:LANES] = v.astype(bf16)
        return c

    lax.fori_loop(0, SEQ // rows_a, prep, 0, unroll=4)

    quarter = SEQ // 4
    streams = ((qf_ref, q4f_ref, q4_ref, q16_ref), (ko_ref, k4f_ref, k4_ref, k16_ref),
               (v_ref, v4f_ref, v4_ref, v16_ref))
    for r in range(4):
        src = pl.ds(r, quarter, stride=4)
        dst = pl.ds(r * quarter, quarter)
        for tok_ref, d4f_ref, d4_ref, _ in streams:
            x = tok_ref[src, :]
            d4f_ref[dst, :] = x
            d4_ref[dst, 0:LANES] = x.astype(bf16)
    for r16 in range(16):
        src = pl.ds((r16 % 4) * quarter + r16 // 4, CHUNK, stride=4)
        dst = pl.ds(r16 * CHUNK, CHUNK)
        for _, d4f_ref, _, d16_ref in streams:
            d16_ref[dst, 0:LANES] = d4f_ref[src, :].astype(bf16)

    def first_block(qd, kd, vd, outs, base, dst):
        rows = pl.ds(base, CHUNK)
        res = _attn_block(qd, kd, vd, rows, rows, bfirst_ref, lane_lo)
        for ref, val in zip(outs, res):
            ref[dst, :] = val

    def band_block(qd, kd, vd, outs, q0, dst):
        k_rows = pl.ds(q0 - CHUNK, 2 * CHUNK)
        res = _attn_block(qd, kd, vd, pl.ds(q0, CHUNK), k_rows, bband_ref, lane_lo)
        for ref, val in zip(outs, res):
            ref[dst, :] = val

    p1 = (q1_ref, k1_ref, v1_ref, (o1_ref, m1_ref, l1_ref))
    p4 = (q4_ref, k4_ref, v4_ref, (o4_ref, m4_ref, l4_ref))
    p16 = (q16_ref, k16_ref, v16_ref, (o16_ref, m16_ref, l16_ref))

    first_block(*p1, 0, pl.ds(0, CHUNK))
    for r in range(4):
        first_block(*p4, r * quarter, pl.ds(r * quarter, CHUNK))

    for c in range(1, SEQ // CHUNK):
        band_block(*p1, c * CHUNK, pl.ds(c * CHUNK, CHUNK))
    for r in range(4):
        for c in range(1, quarter // CHUNK):
            q0 = r * quarter + c * CHUNK
            band_block(*p4, q0, pl.ds(q0, CHUNK))
    for r16 in range(16):
        dst = pl.ds((r16 % 4) * quarter + r16 // 4, CHUNK, stride=4)
        first_block(*p16, r16 * CHUNK, dst)

    for r in range(4):
        def fin(mb, c, r=r):
            sl = pl.ds(pl.multiple_of(r * quarter + mb * CHUNK, CHUNK), CHUNK)
            tok = pl.ds(r + 4 * CHUNK * mb, CHUNK, stride=4)
            m1, m4, m16 = m1_ref[tok, :], m4_ref[sl, :], m16_ref[sl, :]
            mx = jnp.maximum(jnp.maximum(m1, m4), m16)
            a1, a4, a16 = jnp.exp(m1 - mx), jnp.exp(m4 - mx), jnp.exp(m16 - mx)
            num = o1_ref[tok, :] * a1 + o4_ref[sl, :] * a4 + o16_ref[sl, :] * a16
            den = l1_ref[tok, :] * a1 + l4_ref[sl, :] * a4 + l16_ref[sl, :] * a16
            att_ref[tok, :] = num / den
            return c

        lax.fori_loop(0, quarter // CHUNK, fin, 0, unroll=2)


def _attention_prompt(proj3, cos, sin, qg, kg, e2, bias_band, bias_first, layer, k_all, v_all):
    nhp = D_ATT // LANES
    blk = lambda off: pl.BlockSpec((None, SEQ, LANES), lambda b, h: (b, 0, off + h))
    tab = pl.BlockSpec((SEQ, LANES), lambda b, h: (0, 0))
    gain = pl.BlockSpec((None, 1, LANES), lambda b, h: (layer, 0, 0))
    const = lambda shape: pl.BlockSpec(shape, lambda b, h: (0, 0))
    stacked = pl.BlockSpec((None, None, SEQ, LANES), lambda b, h: (layer, b, 0, h))
    hbm = pl.BlockSpec(memory_space=pl.ANY)
    bscr = lambda w: pltpu.VMEM((SEQ, w), bf16)
    fscr = lambda: pltpu.VMEM((SEQ, LANES), f32)
    n_in = 10
    return pl.pallas_call(
        _attn_kernel,
        grid=(BATCH, nhp),
        in_specs=[blk(0), blk(nhp), blk(2 * nhp), tab, tab, gain, gain, const((LANES, LANES)),
                  const((2 * CHUNK, 2 * CHUNK)), const((2 * CHUNK, CHUNK)), hbm, hbm],
        out_specs=[pl.BlockSpec((None, SEQ, LANES), lambda b, h: (b, 0, h)), stacked, stacked],
        out_shape=[jax.ShapeDtypeStruct((BATCH, SEQ, D_ATT), f32),
                   jax.ShapeDtypeStruct((DEPTH, BATCH, SEQ, D_ATT), f32),
                   jax.ShapeDtypeStruct((DEPTH, BATCH, SEQ, D_ATT), f32)],
        scratch_shapes=([fscr() for _ in range(4)] + [bscr(LANES), bscr(LANES), bscr(2 * LANES)] * 3
                        + [fscr() for _ in range(9)]),
        input_output_aliases={n_in: 1, n_in + 1: 2},
        compiler_params=_cparams(("arbitrary", "arbitrary")),
        name="attn_prompt",
    )(proj3, proj3, proj3, cos, sin, qg, kg, e2, bias_band, bias_first, k_all, v_all)


def _attn_biases():
    qi = (np.arange(2 * CHUNK) % CHUNK)[:, None]
    ki = np.arange(2 * CHUNK)[None, :]
    band = np.where((ki >= qi) & (ki <= qi + CHUNK), 0.0, -np.inf).astype(np.float32)
    first = np.where(ki[:, :CHUNK] <= qi, 0.0, -np.inf).astype(np.float32)
    return band, first


GRP = 4 * HD
N_CACHED = MAX_WINDOW
KPAD = N_CACHED + LANES


def _attn_sample_kernel(q_ref, k_ref, v_ref, ck_ref, cv_ref, cos_ref, sin_ref,
                        qg_ref, kg_ref, e2_ref, w_ref, att_ref, kn_ref, kc_ref, vc_ref):
    lane = lax.broadcasted_iota(jnp.int32, (S_STEPS, LANES), 1)
    low_half = (lane % HD) < (HD // 2)
    e2 = e2_ref[...]
    qs, ks = [], []
    for t in range(GRP // LANES):
        cs = slice(t * LANES, (t + 1) * LANES)
        cos = cos_ref[:, cs]
        sin = sin_ref[:, cs]
        qs.append(_head_norm_rope(q_ref[:, cs], qg_ref[...], cos, sin, e2, low_half) * (HD ** -0.5))
        ks.append(_head_norm_rope(k_ref[:, cs], kg_ref[...], cos, sin, e2, low_half))
    q = jnp.concatenate(qs, axis=1)
    k = jnp.concatenate(ks, axis=1)
    kn_ref[...] = k

    kc_ref[0:N_CACHED, :] = ck_ref[...].astype(bf16)
    vc_ref[0:N_CACHED, :] = cv_ref[...].astype(bf16)
    kc_ref[N_CACHED:KPAD, :] = jnp.zeros((KPAD - N_CACHED, GRP), bf16)
    vc_ref[N_CACHED:KPAD, :] = jnp.zeros((KPAD - N_CACHED, GRP), bf16)
    kc_ref[N_CACHED:N_CACHED + 2 * S_STEPS, :] = jnp.concatenate(
        [k, jnp.zeros_like(k)], axis=0).astype(bf16)
    vc_ref[N_CACHED:N_CACHED + 2 * S_STEPS, :] = jnp.concatenate(
        [v_ref[...], jnp.zeros_like(k)], axis=0).astype(bf16)

    nq = (GRP // HD) * S_STEPS
    qt = jnp.concatenate([q] * (GRP // HD) + [jnp.zeros((LANES - nq, GRP), f32)], axis=0)
    row_h = lax.broadcasted_iota(jnp.int32, (LANES, GRP), 0) // S_STEPS
    lane_h = lax.broadcasted_iota(jnp.int32, (LANES, GRP), 1) // HD
    same_head = row_h == lane_h
    qt = jnp.where(same_head, qt, 0.0).astype(bf16)
    s = _dot_nt(kc_ref[...], qt)
    w = w_ref[...]
    keep = w > 0.0
    m = jnp.max(jnp.where(keep, s, -jnp.inf), axis=0, keepdims=True)
    e = jnp.where(keep, w * jnp.exp(s - m), 0.0)
    den = jnp.sum(e, axis=0, keepdims=True)
    pt = jnp.transpose(e / den).astype(bf16)
    res = _dot(pt, vc_ref[...])
    res = jnp.where(same_head, res, 0.0)
    out = res[0:S_STEPS]
    for h in range(1, GRP // HD):
        out = out + res[h * S_STEPS:(h + 1) * S_STEPS]
    att_ref[...] = out


def _attention_sample(proj, ck, cv, cos, sin, qg, kg, e2, wmask, layer):
    ng = D_ATT // GRP
    col = lambda off: pl.BlockSpec((S_STEPS, GRP), lambda b, g: (b, off + g))
    cache = pl.BlockSpec((None, None, N_CACHED, GRP), lambda b, g: (layer, b, 0, g))
    tab = pl.BlockSpec((S_STEPS, GRP), lambda b, g: (0, 0))
    gain = pl.BlockSpec((None, 1, LANES), lambda b, g: (layer, 0, 0))
    return pl.pallas_call(
        _attn_sample_kernel,
        grid=(DEC_BATCH, ng),
        in_specs=[col(0), col(ng), col(2 * ng), cache, cache, tab, tab, gain, gain,
                  pl.BlockSpec((LANES, LANES), lambda b, g: (0, 0)),
                  pl.BlockSpec((KPAD, LANES), lambda b, g: (0, 0))],
        out_specs=[pl.BlockSpec((S_STEPS, GRP), lambda b, g: (b, g))] * 2,
        out_shape=[jax.ShapeDtypeStruct((S_ROWS, D_ATT), f32)] * 2,
        scratch_shapes=[pltpu.VMEM((KPAD, GRP), bf16), pltpu.VMEM((KPAD, GRP), bf16)],
        compiler_params=_cparams(("arbitrary", "arbitrary")),
        name="attn_sample",
    )(proj, proj, proj, ck, cv, cos, sin, qg, kg, e2, wmask)


def _sample_key_weights():
    pos = np.full((KPAD,), -1, np.int64)
    pos[:N_CACHED + DEC_SEQ] = np.arange(N_CACHED + DEC_SEQ)
    w = np.zeros((KPAD, LANES), np.float32)
    for t in range(S_STEPS):
        mult = np.zeros((KPAD,), np.float32)
        if t < DEC_SEQ:
            dist = np.where(pos >= 0, MAX_WINDOW + t - pos, -1)
            for d in DILATIONS:
                mult += ((dist >= 0) & (dist % d == 0) & (dist <= WIN_KEYS * d)).astype(np.float32)
        else:
            mult[N_CACHED + t] = 1.0
        for h in range(GRP // HD):
            w[:, h * S_STEPS + t] = mult
    w[0, (GRP // HD) * S_STEPS:] = 1.0
    return w


def _ssd_kernel(z_ref, xbc_ref, dt_ref, ci_ref, h0_ref, cw_ref, cb_ref, dtb_ref, a_ref, dsk_ref,
                ng_ref, tri_ref, exp_ref, y_ref, ho_ref,
                cbuf_ref, xc_ref, st_ref, zb_ref, db_ref, yb_ref,
                *, padded, valid, has_h0):
    c = pl.program_id(1)
    nc = pl.num_programs(1)

    if padded:
        zb_ref[...] = jnp.zeros_like(zb_ref)
        db_ref[...] = jnp.zeros_like(db_ref)
        cbuf_ref[SUBLANES:, :] = jnp.zeros((CHUNK, D_XBC), f32)
        zb_ref[0:S_STEPS, :] = z_ref[...]
        db_ref[0:S_STEPS, :] = dt_ref[...]
        cbuf_ref[SUBLANES:SUBLANES + S_STEPS, :] = xbc_ref[...]
        zsrc, dsrc = zb_ref, db_ref
    else:
        cbuf_ref[SUBLANES:, :] = xbc_ref[...]
        zsrc, dsrc = z_ref, dt_ref

    @pl.when(c == 0)
    def _():
        cbuf_ref[0:SUBLANES, :] = ci_ref[...]
        if has_h0:
            st_ref[...] = jnp.transpose(h0_ref[...])
        else:
            st_ref[...] = jnp.zeros_like(st_ref)

    for t in range(D_XBC // 256):
        cs = slice(t * 256, (t + 1) * 256)
        acc = cb_ref[:, cs] + cbuf_ref[5:5 + CHUNK, cs] * cw_ref[0:1, cs]
        acc = acc + cbuf_ref[6:6 + CHUNK, cs] * cw_ref[1:2, cs]
        acc = acc + cbuf_ref[7:7 + CHUNK, cs] * cw_ref[2:3, cs]
        acc = acc + cbuf_ref[8:8 + CHUNK, cs] * cw_ref[3:4, cs]
        xc_ref[:, cs] = _silu(acc)
    cbuf_ref[0:SUBLANES, :] = cbuf_ref[CHUNK:CHUNK + SUBLANES, :]

    tri = tri_ref[...]
    expand = exp_ref[...]
    dt = _softplus(dsrc[...] + dtb_ref[...])
    a = dt * a_ref[...]
    a_cum = None
    for hi in _split(a, 3):
        t_ = _dot(tri, hi)
        a_cum = t_ if a_cum is None else a_cum + t_
    a_cum_t = jnp.transpose(a_cum)
    dt_e = _split_dot(dt, expand, 2)
    acum_e = _split_dot(a_cum, expand, 3)
    alast_e = acum_e[valid - 1:valid, :]
    row = lax.broadcasted_iota(jnp.int32, (CHUNK, D_SSD), 0)

    xs = xc_ref[:, 0:D_SSD]
    xdt = xs * dt_e
    xdt_b = xdt.astype(bf16)
    xend_b = jnp.where(row < valid, xdt * jnp.exp(alast_e - acum_e), 0.0).astype(bf16)

    ii = lax.broadcasted_iota(jnp.int32, (CHUNK, CHUNK), 0)
    jj = lax.broadcasted_iota(jnp.int32, (CHUNK, CHUNK), 1)
    causal = jj <= ii
    lane_lo = lax.broadcasted_iota(jnp.int32, (CHUNK, LANES), 1) < HD
    hpg = H_S // SSD_GROUPS
    gw = hpg * HD
    for g in range(SSD_GROUPS):
        gs = slice(g * gw, (g + 1) * gw)
        b0 = D_SSD + g * SSD_STATE
        c0 = D_SSD + (SSD_GROUPS + g) * SSD_STATE
        bm = xc_ref[:, b0:b0 + SSD_STATE]
        cm = xc_ref[:, c0:c0 + SSD_STATE].astype(bf16)
        cbm = _dot_nt(cm, bm.astype(bf16))
        bt = jnp.transpose(bm).astype(bf16)
        st_new = _dot(bt, xend_b[:, gs])
        y_off = _dot(cm, st_ref[:, gs].astype(bf16))
        for pr in range(hpg // 2):
            ys = []
            ps = slice(g * gw + pr * LANES, g * gw + (pr + 1) * LANES)
            for hh in range(2):
                h = g * hpg + pr * 2 + hh
                seg = a_cum[:, h:h + 1] - a_cum_t[h:h + 1, :]
                gm = (cbm * jnp.exp(jnp.where(causal, seg, -jnp.inf))).astype(bf16)
                ys.append(_dot(gm, xdt_b[:, ps]))
            yb_ref[:, ps] = jnp.where(lane_lo, ys[0], ys[1])
        yb_ref[:, gs] = yb_ref[:, gs] + y_off * jnp.exp(acum_e[:, gs])
        st_ref[:, gs] = jnp.exp(alast_e[:, gs]) * st_ref[:, gs] + st_new

    y = yb_ref[...] + dsk_ref[...] * xs
    y = y * _silu(zsrc[...])
    ms = jnp.mean(y * y, axis=-1, keepdims=True)
    y = y * lax.rsqrt(ms + EPS) * ng_ref[...]
    if padded:
        y_ref[...] = y[0:S_STEPS]
    else:
        y_ref[...] = y.astype(y_ref.dtype)

    @pl.when(c == nc - 1)
    def _():
        ho_ref[...] = jnp.transpose(st_ref[...])


def _ssd(proj, dt_raw, cinit, h0, cw, cb, dtb, a_neg, dskip, ng, tri, expand, layer, *, sample):
    if sample:
        nb, nc, rows = DEC_BATCH, 1, S_STEPS
        y_dtype = f32
        h0_spec = pl.BlockSpec((None, None, D_SSD, SSD_STATE), lambda b, c: (layer, b, 0, 0))
    else:
        nb, nc, rows = BATCH, SEQ // CHUNK, CHUNK
        y_dtype = bf16
        h0_spec = pl.BlockSpec((None, None, D_SSD, SSD_STATE), lambda b, c: (0, 0, 0, 0))
    rowblk = lambda width, colblk: pl.BlockSpec((rows, width), lambda b, c: (b * nc + c, colblk))
    vec = lambda width: pl.BlockSpec((None, 1, width), lambda b, c: (layer, 0, 0))
    const = lambda shape: pl.BlockSpec(shape, lambda b, c: (0, 0))
    kern = functools.partial(_ssd_kernel, padded=sample, valid=DEC_SEQ if sample else CHUNK,
                             has_h0=sample)
    return pl.pallas_call(
        kern,
        grid=(nb, nc),
        in_specs=[
            rowblk(D_SSD, COL_Z // D_SSD),
            rowblk(D_XBC, COL_XBC // D_XBC),
            rowblk(LANES, 0),
            pl.BlockSpec((None, SUBLANES, D_XBC), lambda b, c: (b, 0, 0)),
            h0_spec,
            pl.BlockSpec((None, SSD_CONV, D_XBC), lambda b, c: (layer, 0, 0)),
            vec(D_XBC), vec(LANES), vec(LANES), vec(D_SSD), vec(D_SSD),
            const((CHUNK, CHUNK)), const((LANES, D_SSD)),
        ],
        out_specs=[rowblk(D_SSD, 0), pl.BlockSpec((None, D_SSD, SSD_STATE), lambda b, c: (b, 0, 0))],
        out_shape=[jax.ShapeDtypeStruct((nb * nc * rows, D_SSD), y_dtype),
                   jax.ShapeDtypeStruct((nb, D_SSD, SSD_STATE), f32)],
        scratch_shapes=[
            pltpu.VMEM((CHUNK + SUBLANES, D_XBC), f32),
            pltpu.VMEM((CHUNK, D_XBC), f32),
            pltpu.VMEM((SSD_STATE, D_SSD), f32),
            pltpu.VMEM((CHUNK, D_SSD), f32),
            pltpu.VMEM((CHUNK, LANES), f32),
            pltpu.VMEM((CHUNK, D_SSD), f32),
        ],
        compiler_params=_cparams(("arbitrary", "arbitrary")),
        name="ssd_sample" if sample else "ssd_prompt",
    )(proj, proj, dt_raw, cinit, h0, cw, cb, dtb, a_neg, dskip, ng, tri, expand)


def _rope_tables(pos, width):
    half = HD // 2
    inv = ROPE_THETA ** (-jnp.arange(half, dtype=f32) / half)
    ang = pos.astype(f32)[:, None] * inv[None, :]
    cos = jnp.cos(ang)
    sin = jnp.sin(ang)
    cos_h = jnp.concatenate([cos, cos], axis=-1)
    sin_h = jnp.concatenate([-sin, sin], axis=-1)
    reps = width // HD
    return jnp.tile(cos_h, (1, reps)), jnp.tile(sin_h, (1, reps))


def kernel(x_prompt, x_sample, cache_win_k, cache_win_v, state_ssd_conv, state_ssd, state_ffn_conv,
           norm1_g, w_in, q_norm_g, k_norm_g, ssd_conv_w, ssd_conv_b, ssd_dt_bias, ssd_a_log,
           ssd_d, ssd_norm_g, w_out, norm2_g, w_up, ffn_conv_w, ffn_conv_b, w_down):
    w_in_b = w_in.astype(bf16)
    w_dt_b = jnp.pad(w_in[:, :, D_PROJ:], ((0, 0), (0, 0), (0, LANES - H_S))).astype(bf16)
    w_out_b = w_out.astype(bf16)
    w_up_b = w_up.astype(bf16)
    w_down_b = w_down.astype(bf16)
    g1 = norm1_g[:, None, :]
    g2 = norm2_g[:, None, :]
    qg = jnp.tile(q_norm_g, (1, LANES // HD))[:, None, :]
    kg = jnp.tile(k_norm_g, (1, LANES // HD))[:, None, :]
    cb_ssd = ssd_conv_b[:, None, :]
    lane_pad = ((0, 0), (0, LANES - H_S))
    dtb = jnp.pad(ssd_dt_bias, lane_pad)[:, None, :]
    a_neg = jnp.pad(-jnp.exp(ssd_a_log.astype(f32)), lane_pad)[:, None, :]
    dskip = jnp.repeat(ssd_d, HD, axis=1)[:, None, :]
    ng = ssd_norm_g[:, None, :]
    cb_ffn = ffn_conv_b[:, None, :]

    idx = np.arange(LANES)
    e2 = jnp.asarray((idx[:, None] // HD == idx[None, :] // HD).astype(np.float32), dtype=bf16)
    tri = jnp.asarray((idx[None, :] <= idx[:, None]).astype(np.float32), dtype=bf16)
    expand = jnp.asarray((idx[:, None] == np.arange(D_SSD)[None, :] // HD).astype(np.float32), dtype=bf16)
    wmask = jnp.asarray(_sample_key_weights())
    bias_band, bias_first = (jnp.asarray(a) for a in _attn_biases())
    cos_p, sin_p = _rope_tables(jnp.arange(SEQ), LANES)
    cos_s, sin_s = _rope_tables(PAST_LEN + jnp.arange(S_STEPS), GRP)

    xp = x_prompt.reshape(BATCH * SEQ, D_MODEL)
    xs = jnp.pad(x_sample, ((0, 0), (0, S_STEPS - DEC_SEQ), (0, 0))).reshape(S_ROWS, D_MODEL)

    ck = cache_win_k.reshape(DEPTH, DEC_BATCH, MAX_WINDOW, D_ATT)
    cv = cache_win_v.reshape(DEPTH, DEC_BATCH, MAX_WINDOW, D_ATT)
    ssd_ci_p = jnp.zeros((BATCH, SUBLANES, D_XBC), f32)
    ssd_ci_s = jnp.pad(state_ssd_conv, ((0, 0), (0, 0), (SUBLANES - (SSD_CONV - 1), 0), (0, 0)))
    h0_s = state_ssd.reshape(DEPTH, DEC_BATCH, D_SSD, SSD_STATE)
    h0_p = jnp.zeros((1, 1, D_SSD, SSD_STATE), f32)
    ffn_ci_p = jnp.zeros((BATCH, FFN_PAD, 2 * D_FF), f32)
    ffn_ci_s = jnp.pad(state_ffn_conv, ((0, 0), (0, 0), (SUBLANES - (FFN_CONV - 1), S_STEPS), (0, 0))
                       ).reshape(DEPTH, 1, DEC_BATCH * FFN_PAD, 2 * D_FF)

    tm_in, tm_p, tm_ffn = 1024, 512, 1024
    k_all = jnp.zeros((DEPTH, BATCH, SEQ, D_ATT), f32)
    v_all = jnp.zeros((DEPTH, BATCH, SEQ, D_ATT), f32)
    outs = {k: [] for k in ("ks", "vs", "cp", "cs", "hp", "hs", "fp", "fs")}
    for i in range(DEPTH):
        proj, dt_raw = _in_proj(xp, g1, w_in_b, w_dt_b, i, tm_in)
        proj3 = proj.reshape(BATCH, SEQ, D_PROJ)
        att, k_all, v_all = _attention_prompt(proj3, cos_p, sin_p, qg, kg, e2, bias_band, bias_first, i,
                                              k_all, v_all)
        ssd, h_last = _ssd(proj, dt_raw, ssd_ci_p, h0_p, ssd_conv_w, cb_ssd, dtb, a_neg, dskip, ng, tri,
                           expand, i, sample=False)
        x1 = _out_proj(att.reshape(BATCH * SEQ, D_ATT), ssd, w_out_b, xp, i, tm_ffn, tn=D_MODEL // 2)
        xp, hs_g, hs_u = _ffn(x1, g2, w_up_b, ffn_conv_w, cb_ffn, w_down_b, ffn_ci_p, i, tm=tm_ffn,
                              tiles_per_seq=SEQ // tm_ffn, per_tile=False)
        outs["cp"].append(proj3[:, SEQ - (SSD_CONV - 1):, COL_XBC:COL_XBC + D_XBC])
        outs["hp"].append(h_last.reshape(BATCH, H_S, HD, SSD_STATE))
        last = slice(SEQ // tm_ffn - 1, None, SEQ // tm_ffn)
        hs = jnp.concatenate([hs_g[last], hs_u[last]], axis=-1)
        outs["fp"].append(hs[:, FFN_PAD - (FFN_CONV - 1):, :])

        proj_s, dt_s = _in_proj(xs, g1, w_in_b, w_dt_b, i, S_ROWS, tn=D_PROJ // 2)
        att_s, kn_s = _attention_sample(proj_s, ck, cv, cos_s, sin_s, qg, kg, e2, wmask, i)
        ssd_s, h_last_s = _ssd(proj_s, dt_s, ssd_ci_s[i], h0_s, ssd_conv_w, cb_ssd, dtb, a_neg, dskip, ng,
                               tri, expand, i, sample=True)
        x1_s = _out_proj(att_s, ssd_s, w_out_b, xs, i, S_ROWS)
        xs, hs_g, hs_u = _ffn(x1_s, g2, w_up_b, ffn_conv_w, cb_ffn, w_down_b, ffn_ci_s[i], i,
                              tm=S_ROWS, tiles_per_seq=1, per_tile=True)
        p3 = proj_s.reshape(DEC_BATCH, S_STEPS, D_PROJ)
        outs["ks"].append(kn_s.reshape(DEC_BATCH, S_STEPS, H_A, HD)[:, :DEC_SEQ])
        outs["vs"].append(p3[:, :DEC_SEQ, 2 * D_ATT:3 * D_ATT].reshape(DEC_BATCH, DEC_SEQ, H_A, HD))
        outs["cs"].append(p3[:, DEC_SEQ - (SSD_CONV - 1):DEC_SEQ, COL_XBC:COL_XBC + D_XBC])
        outs["hs"].append(h_last_s.reshape(DEC_BATCH, H_S, HD, SSD_STATE))
        hs = jnp.concatenate([hs_g[0], hs_u[0]], axis=-1).reshape(DEC_BATCH, FFN_PAD, 2 * D_FF)
        lo = SUBLANES + DEC_SEQ - (FFN_CONV - 1)
        outs["fs"].append(hs[:, lo:lo + FFN_CONV - 1, :])

    yp = xp.reshape(BATCH, SEQ, D_MODEL)
    ys = xs.reshape(DEC_BATCH, S_STEPS, D_MODEL)[:, :DEC_SEQ]
    st = lambda k: jnp.stack(outs[k])
    win_shape = (DEPTH, BATCH, SEQ, H_A, HD)
    return (yp, ys, k_all.reshape(win_shape), v_all.reshape(win_shape), st("ks"), st("vs"),
            st("cp"), st("cs"), st("hp"), st("hs"), st("fp"), st("fs"))
```

```python
import functools

import numpy as np
import jax
import jax.numpy as jnp
from jax import lax
from jax.experimental import pallas as pl
from jax.experimental.pallas import tpu as pltpu

f32 = jnp.float32
bf16 = jnp.bfloat16

D_MODEL = 2048
BATCH = 4
SEQ = 2048
DEPTH = 4
DEC_BATCH = 8
DEC_SEQ = 4
PAST_LEN = 16384
HD = 64
D_ATT = 1024
H_A = 16
DILATIONS = (1, 4, 16)
WIN_KEYS = 128
MAX_WINDOW = 2048
ROPE_THETA = 10000.0
D_SSD = 1024
H_S = 16
SSD_STATE = 128
SSD_GROUPS = 4
SSD_CONV = 4
D_XBC = 2048
D_FF = 5632
FFN_CONV = 3
EPS = 1e-6

LANES = 128
SUBLANES = 8
CHUNK = 128
S_STEPS = SUBLANES
S_ROWS = S_STEPS * DEC_BATCH

COL_Z = 3 * D_ATT
COL_XBC = COL_Z + D_SSD
D_PROJ = COL_XBC + D_XBC

VMEM_LIMIT = 56 * 1024 * 1024


def _cparams(sem):
    return pltpu.CompilerParams(dimension_semantics=sem, vmem_limit_bytes=VMEM_LIMIT)


def _dot(a, b):
    return jnp.dot(a, b, preferred_element_type=f32)


def _dot_nt(a, b):
    return lax.dot_general(a, b, (((1,), (1,)), ((), ())), preferred_element_type=f32)


def _split(x, parts):
    out = []
    rem = x
    for p in range(parts):
        hi = rem.astype(bf16)
        out.append(hi)
        if p + 1 < parts:
            rem = rem - hi.astype(f32)
    return out


def _split_dot(x, w, parts):
    acc = None
    for hi in _split(x, parts):
        t = _dot(hi, w)
        acc = t if acc is None else acc + t
    return acc


def _silu(x):
    return x * (1.0 / (1.0 + jnp.exp(-x)))


def _softplus(x):
    return jnp.maximum(x, 0.0) + jnp.log1p(jnp.exp(-jnp.abs(x)))


def _rmsnorm_to(x_ref, g_ref, xn_ref, rows):
    chunk = min(rows, 256)

    def body(i, c):
        r = pl.multiple_of(i * chunk, chunk)
        x = x_ref[pl.ds(r, chunk), :]
        ms = jnp.mean(x * x, axis=-1, keepdims=True)
        xn_ref[pl.ds(r, chunk), :] = (x * lax.rsqrt(ms + EPS) * g_ref[...]).astype(bf16)
        return c

    lax.fori_loop(0, rows // chunk, body, 0)


def _inproj_kernel(x_ref, g_ref, w_ref, wdt_ref, o_ref, dt_ref, xn_ref, *, tm):
    @pl.when(pl.program_id(1) == 0)
    def _():
        _rmsnorm_to(x_ref, g_ref, xn_ref, tm)
        dt_ref[...] = _dot(xn_ref[...], wdt_ref[...])

    o_ref[...] = _dot(xn_ref[...], w_ref[...])


def _in_proj(x, g, w, wdt, layer, tm, tn=1536):
    rows = x.shape[0]
    return pl.pallas_call(
        functools.partial(_inproj_kernel, tm=tm),
        grid=(rows // tm, D_PROJ // tn),
        in_specs=[
            pl.BlockSpec((tm, D_MODEL), lambda i, j: (i, 0)),
            pl.BlockSpec((None, 1, D_MODEL), lambda i, j: (layer, 0, 0)),
            pl.BlockSpec((None, D_MODEL, tn), lambda i, j: (layer, 0, j)),
            pl.BlockSpec((None, D_MODEL, LANES), lambda i, j: (layer, 0, 0)),
        ],
        out_specs=[pl.BlockSpec((tm, tn), lambda i, j: (i, j)),
                   pl.BlockSpec((tm, LANES), lambda i, j: (i, 0))],
        out_shape=[jax.ShapeDtypeStruct((rows, D_PROJ), f32),
                   jax.ShapeDtypeStruct((rows, LANES), f32)],
        scratch_shapes=[pltpu.VMEM((tm, D_MODEL), bf16)],
        compiler_params=_cparams(("arbitrary", "arbitrary")),
        name="in_proj",
    )(x, g, w, wdt)


def _outproj_kernel(a_ref, s_ref, wa_ref, ws_ref, x_ref, o_ref):
    acc = _dot(a_ref[...].astype(bf16), wa_ref[...])
    acc = acc + _dot(s_ref[...].astype(bf16), ws_ref[...])
    o_ref[...] = x_ref[...] + acc


def _out_proj(att, ssd, w, x, layer, tm, tn=D_MODEL):
    rows = x.shape[0]
    return pl.pallas_call(
        _outproj_kernel,
        grid=(rows // tm, D_MODEL // tn),
        in_specs=[
            pl.BlockSpec((tm, D_ATT), lambda i, j: (i, 0)),
            pl.BlockSpec((tm, D_SSD), lambda i, j: (i, 0)),
            pl.BlockSpec((None, D_ATT, tn), lambda i, j: (layer, 0, j)),
            pl.BlockSpec((None, D_SSD, tn), lambda i, j: (layer, 1, j)),
            pl.BlockSpec((tm, tn), lambda i, j: (i, j)),
        ],
        out_specs=pl.BlockSpec((tm, tn), lambda i, j: (i, j)),
        out_shape=jax.ShapeDtypeStruct((rows, D_MODEL), f32),
        compiler_params=_cparams(("arbitrary", "arbitrary")),
        name="out_proj",
    )(att, ssd, w, w, x)


FFN_PAD = 2 * SUBLANES


def _ffn_kernel(x_ref, g_ref, wg_ref, wu_ref, cwg_ref, cwu_ref, cbg_ref, cbu_ref, wd_ref,
                cig_ref, ciu_ref, o_ref, hsg_ref, hsu_ref,
                xn_ref, hbg_ref, hbu_ref, cag_ref, cau_ref, act_ref,
                *, tm, tiles_per_seq, per_tile, rc):
    i = pl.program_id(0)
    j = pl.program_id(1)
    pad = FFN_PAD
    gate = (wg_ref, cig_ref, hbg_ref, cag_ref, hsg_ref, cwg_ref, cbg_ref)
    up = (wu_ref, ciu_ref, hbu_ref, cau_ref, hsu_ref, cwu_ref, cbu_ref)

    @pl.when(j == 0)
    def _():
        _rmsnorm_to(x_ref, g_ref, xn_ref, tm)
        o_ref[...] = x_ref[...]

    def conv_act(r, out_rows):
        convs = []
        for _, _, hb_ref, _, _, cw_ref, cb_ref in (gate, up):
            y = cb_ref[...] + hb_ref[r - 2:r - 2 + rc, :] * cw_ref[0:1, :]
            y = y + hb_ref[r - 1:r - 1 + rc, :] * cw_ref[1:2, :]
            y = y + hb_ref[r:r + rc, :] * cw_ref[2:3, :]
            convs.append(y)
        act_ref[out_rows, :] = (_silu(convs[0]) * convs[1]).astype(bf16)

    if per_tile:
        for w_ref, ci_ref, hb_ref, _, hs_ref, _, _ in (gate, up):
            hb_ref[...] = ci_ref[...]
            h = _dot(xn_ref[...], w_ref[...])
            for b in range(tm // SUBLANES):
                hb_ref[b * pad + SUBLANES:(b + 1) * pad, :] = h[b * SUBLANES:(b + 1) * SUBLANES, :]
            hs_ref[...] = hb_ref[...]
        for b in range(tm // SUBLANES):
            conv_act(b * pad + SUBLANES, slice(b * SUBLANES, (b + 1) * SUBLANES))
        o_ref[...] += _dot(act_ref[...], wd_ref[...])
    else:
        first = (i % tiles_per_seq) == 0
        for _, ci_ref, hb_ref, ca_ref, _, _, _ in (gate, up):
            hb_ref[0:pad, :] = jnp.where(first, ci_ref[...], ca_ref[j])

        half = tm // 2
        halves = (slice(0, half), slice(half, tm))

        def up_piece(rows, branch):
            w_ref, hb_ref = branch[0], branch[2]
            hb_ref[pad + rows.start:pad + rows.stop, :] = _dot(xn_ref[rows, :], w_ref[...])

        def down_piece(rows, cols):
            o_ref[rows, cols] += _dot(act_ref[rows, :], wd_ref[:, cols])

        def conv_rows(r0, n):
            for c in range(n):
                conv_act(pad + r0 + c * rc, slice(r0 + c * rc, r0 + (c + 1) * rc))

        per = half // rc // 2
        lo, hi = slice(0, D_MODEL // 2), slice(D_MODEL // 2, D_MODEL)
        up_piece(halves[0], gate)
        up_piece(halves[0], up)
        up_piece(halves[1], gate)
        conv_rows(0, per)
        up_piece(halves[1], up)
        conv_rows(per * rc, per)
        down_piece(halves[0], lo)
        conv_rows(half, per)
        down_piece(halves[0], hi)
        conv_rows(half + per * rc, per)
        down_piece(halves[1], lo)
        down_piece(halves[1], hi)
        for _, _, hb_ref, ca_ref, hs_ref, _, _ in (gate, up):
            ca_ref[j] = hb_ref[tm:tm + pad, :]
            hs_ref[...] = hb_ref[tm:tm + pad, :]


def _ffn(x, g, w_up, cw, cb, w_down, cin, layer, *, tm, tiles_per_seq, per_tile, tf=512):
    rows = x.shape[0]
    nj = D_FF // tf
    pad = FFN_PAD
    rc = SUBLANES if per_tile else min(tm, 64)
    hb_rows = (tm // SUBLANES) * pad if per_tile else tm + pad
    st_rows = hb_rows if per_tile else pad
    kern = functools.partial(_ffn_kernel, tm=tm, tiles_per_seq=tiles_per_seq, per_tile=per_tile, rc=rc)
    x_mode = {} if per_tile else dict(pipeline_mode=pl.Buffered(1))
    return pl.pallas_call(
        kern,
        grid=(rows // tm, nj),
        in_specs=[
            pl.BlockSpec((tm, D_MODEL), lambda i, j: (i, 0), **x_mode),
            pl.BlockSpec((None, 1, D_MODEL), lambda i, j: (layer, 0, 0)),
            pl.BlockSpec((None, D_MODEL, tf), lambda i, j: (layer, 0, j)),
            pl.BlockSpec((None, D_MODEL, tf), lambda i, j: (layer, 0, nj + j)),
            pl.BlockSpec((None, FFN_CONV, tf), lambda i, j: (layer, 0, j)),
            pl.BlockSpec((None, FFN_CONV, tf), lambda i, j: (layer, 0, nj + j)),
            pl.BlockSpec((None, 1, tf), lambda i, j: (layer, 0, j)),
            pl.BlockSpec((None, 1, tf), lambda i, j: (layer, 0, nj + j)),
            pl.BlockSpec((None, tf, D_MODEL), lambda i, j: (layer, j, 0)),
            pl.BlockSpec((None, st_rows, tf), lambda i, j: (i // tiles_per_seq, 0, j)),
            pl.BlockSpec((None, st_rows, tf), lambda i, j: (i // tiles_per_seq, 0, nj + j)),
        ],
        out_specs=[
            pl.BlockSpec((tm, D_MODEL), lambda i, j: (i, 0)),
            pl.BlockSpec((None, st_rows, tf), lambda i, j: (i, 0, j)),
            pl.BlockSpec((None, st_rows, tf), lambda i, j: (i, 0, j)),
        ],
        out_shape=[
            jax.ShapeDtypeStruct((rows, D_MODEL), f32),
            jax.ShapeDtypeStruct((rows // tm, st_rows, D_FF), f32),
            jax.ShapeDtypeStruct((rows // tm, st_rows, D_FF), f32),
        ],
        scratch_shapes=[
            pltpu.VMEM((tm, D_MODEL), bf16),
            pltpu.VMEM((hb_rows, tf), f32),
            pltpu.VMEM((hb_rows, tf), f32),
            pltpu.VMEM((nj, pad, tf), f32),
            pltpu.VMEM((nj, pad, tf), f32),
            pltpu.VMEM((tm, tf), bf16),
        ],
        compiler_params=_cparams(("arbitrary", "arbitrary")),
        name="conv_ffn",
    )(x, g, w_up, w_up, cw, cw, cb, cb, w_down, cin, cin)


def _head_norm_rope(x, gain, cos, sin, e2, low_half):
    ssq = _split_dot(x * x, e2, 2)
    xn = x * lax.rsqrt(ssq * (1.0 / HD) + EPS) * gain
    rot = jnp.where(low_half, pltpu.roll(xn, LANES - HD // 2, 1), pltpu.roll(xn, HD // 2, 1))
    return xn * cos + rot * sin


def _attn_block(q_ref, k_ref, ve_ref, q_rows, k_rows, bias_ref, lane_lo):
    qb = q_ref[q_rows, :]
    zero = jnp.zeros_like(qb)
    q2 = jnp.concatenate([jnp.where(lane_lo, qb, zero), jnp.where(lane_lo, zero, qb)], axis=0)
    s = _dot_nt(q2, k_ref[k_rows, :]) + bias_ref[...]
    m = jnp.max(s, axis=1, keepdims=True)
    p = jnp.exp(s - m).astype(bf16)
    r = _dot(p, ve_ref[k_rows, :])
    shape = (CHUNK, LANES)
    o = jnp.where(lane_lo, r[:CHUNK, :LANES], r[CHUNK:, :LANES])
    le = jnp.where(lane_lo, r[:CHUNK, LANES:], r[CHUNK:, LANES:])
    me = jnp.where(lane_lo, jnp.broadcast_to(m[:CHUNK], shape), jnp.broadcast_to(m[CHUNK:], shape))
    return o, me, le


def _attn_kernel(q_ref, k_ref, v_ref, cos_ref, sin_ref, qg_ref, kg_ref, e2_ref, bband_ref, bfirst_ref,
                 k_all_ref, v_all_ref,
                 att_ref, ko_ref, vo_ref, qf_ref, q4f_ref, k4f_ref, v4f_ref,
                 q1_ref, k1_ref, v1_ref, q4_ref, k4_ref, v4_ref, q16_ref, k16_ref, v16_ref,
                 o1_ref, m1_ref, l1_ref, o4_ref, m4_ref, l4_ref, o16_ref, m16_ref, l16_ref):
    lane_lo = lax.broadcasted_iota(jnp.int32, (CHUNK, LANES), 1) < HD
    rows_a = 256
    lane_a = lax.broadcasted_iota(jnp.int32, (rows_a, LANES), 1)
    low_half_a = (lane_a % HD) < (HD // 2)

    @pl.when((pl.program_id(0) == 0) & (pl.program_id(1) == 0))
    def _():
        ones = jnp.ones((SEQ, LANES), bf16)
        v1_ref[:, LANES:] = ones
        v4_ref[:, LANES:] = ones
        v16_ref[:, LANES:] = ones

    def prep(i, c):
        r = pl.multiple_of(i * rows_a, rows_a)
        sl = pl.ds(r, rows_a)
        cos = cos_ref[sl, :]
        sin = sin_ref[sl, :]
        q = _head_norm_rope(q_ref[sl, :], qg_ref[...], cos, sin, e2_ref[...], low_half_a) * (HD ** -0.5)
        k = _head_norm_rope(k_ref[sl, :], kg_ref[...], cos, sin, e2_ref[...], low_half_a)
        v = v_ref[sl, :]
        qf_ref[sl, :] = q
        ko_ref[sl, :] = k
        vo_ref[sl, :] = v
        q1_ref[sl, :] = q.astype(bf16)
        k1_ref[sl, :] = k.astype(bf16)
        v1_ref[sl, 0:LANES] = v.astype(bf16)
        return c

    lax.fori_loop(0, SEQ // rows_a, prep, 0, unroll=4)

    quarter = SEQ // 4
    streams = ((qf_ref, q4f_ref, q4_ref, q16_ref), (ko_ref, k4f_ref, k4_ref, k16_ref),
               (v_ref, v4f_ref, v4_ref, v16_ref))
    for r in range(4):
        src = pl.ds(r, quarter, stride=4)
        dst = pl.ds(r * quarter, quarter)
        for tok_ref, d4f_ref, d4_ref, _ in streams:
            x = tok_ref[src, :]
            d4f_ref[dst, :] = x
            d4_ref[dst, 0:LANES] = x.astype(bf16)
    for r16 in range(16):
        src = pl.ds((r16 % 4) * quarter + r16 // 4, CHUNK, stride=4)
        dst = pl.ds(r16 * CHUNK, CHUNK)
        for _, d4f_ref, _, d16_ref in streams:
            d16_ref[dst, 0:LANES] = d4f_ref[src, :].astype(bf16)

    def first_block(qd, kd, vd, outs, base, dst):
        rows = pl.ds(base, CHUNK)
        res = _attn_block(qd, kd, vd, rows, rows, bfirst_ref, lane_lo)
        for ref, val in zip(outs, res):
            ref[dst, :] = val

    def band_block(qd, kd, vd, outs, q0, dst):
        k_rows = pl.ds(q0 - CHUNK, 2 * CHUNK)
        res = _attn_block(qd, kd, vd, pl.ds(q0, CHUNK), k_rows, bband_ref, lane_lo)
        for ref, val in zip(outs, res):
            ref[dst, :] = val

    p1 = (q1_ref, k1_ref, v1_ref, (o1_ref, m1_ref, l1_ref))
    p4 = (q4_ref, k4_ref, v4_ref, (o4_ref, m4_ref, l4_ref))
    p16 = (q16_ref, k16_ref, v16_ref, (o16_ref, m16_ref, l16_ref))

    first_block(*p1, 0, pl.ds(0, CHUNK))
    for r in range(4):
        first_block(*p4, r * quarter, pl.ds(r * quarter, CHUNK))

    for c in range(1, SEQ // CHUNK):
        band_block(*p1, c * CHUNK, pl.ds(c * CHUNK, CHUNK))
    for r in range(4):
        for c in range(1, quarter // CHUNK):
            q0 = r * quarter + c * CHUNK
            band_block(*p4, q0, pl.ds(q0, CHUNK))
    for r16 in range(16):
        dst = pl.ds((r16 % 4) * quarter + r16 // 4, CHUNK, stride=4)
        first_block(*p16, r16 * CHUNK, dst)

    for r in range(4):
        def fin(mb, c, r=r):
            sl = pl.ds(pl.multiple_of(r * quarter + mb * CHUNK, CHUNK), CHUNK)
            tok = pl.ds(r + 4 * CHUNK * mb, CHUNK, stride=4)
            m1, m4, m16 = m1_ref[tok, :], m4_ref[sl, :], m16_ref[sl, :]
            mx = jnp.maximum(jnp.maximum(m1, m4), m16)
            a1, a4, a16 = jnp.exp(m1 - mx), jnp.exp(m4 - mx), jnp.exp(m16 - mx)
            num = o1_ref[tok, :] * a1 + o4_ref[sl, :] * a4 + o16_ref[sl, :] * a16
            den = l1_ref[tok, :] * a1 + l4_ref[sl, :] * a4 + l16_ref[sl, :] * a16
            att_ref[tok, :] = num / den
            return c

        lax.fori_loop(0, quarter // CHUNK, fin, 0, unroll=2)


def _attention_prompt(proj3, cos, sin, qg, kg, e2, bias_band, bias_first, layer, k_all, v_all):
    nhp = D_ATT // LANES
    blk = lambda off: pl.BlockSpec((None, SEQ, LANES), lambda b, h: (b, 0, off + h))
    tab = pl.BlockSpec((SEQ, LANES), lambda b, h: (0, 0))
    gain = pl.BlockSpec((None, 1, LANES), lambda b, h: (layer, 0, 0))
    const = lambda shape: pl.BlockSpec(shape, lambda b, h: (0, 0))
    stacked = pl.BlockSpec((None, None, SEQ, LANES), lambda b, h: (layer, b, 0, h))
    hbm = pl.BlockSpec(memory_space=pl.ANY)
    bscr = lambda w: pltpu.VMEM((SEQ, w), bf16)
    fscr = lambda: pltpu.VMEM((SEQ, LANES), f32)
    n_in = 10
    return pl.pallas_call(
        _attn_kernel,
        grid=(BATCH, nhp),
        in_specs=[blk(0), blk(nhp), blk(2 * nhp), tab, tab, gain, gain, const((LANES, LANES)),
                  const((2 * CHUNK, 2 * CHUNK)), const((2 * CHUNK, CHUNK)), hbm, hbm],
        out_specs=[pl.BlockSpec((None, SEQ, LANES), lambda b, h: (b, 0, h)), stacked, stacked],
        out_shape=[jax.ShapeDtypeStruct((BATCH, SEQ, D_ATT), f32),
                   jax.ShapeDtypeStruct((DEPTH, BATCH, SEQ, D_ATT), f32),
                   jax.ShapeDtypeStruct((DEPTH, BATCH, SEQ, D_ATT), f32)],
        scratch_shapes=([fscr() for _ in range(4)] + [bscr(LANES), bscr(LANES), bscr(2 * LANES)] * 3
                        + [fscr() for _ in range(9)]),
        input_output_aliases={n_in: 1, n_in + 1: 2},
        compiler_params=_cparams(("arbitrary", "arbitrary")),
        name="attn_prompt",
    )(proj3, proj3, proj3, cos, sin, qg, kg, e2, bias_band, bias_first, k_all, v_all)


def _attn_biases():
    qi = (np.arange(2 * CHUNK) % CHUNK)[:, None]
    ki = np.arange(2 * CHUNK)[None, :]
    band = np.where((ki >= qi) & (ki <= qi + CHUNK), 0.0, -np.inf).astype(np.float32)
    first = np.where(ki[:, :CHUNK] <= qi, 0.0, -np.inf).astype(np.float32)
    return band, first


GRP = 4 * HD
N_CACHED = MAX_WINDOW
KPAD = N_CACHED + LANES


def _attn_sample_kernel(q_ref, k_ref, v_ref, ck_ref, cv_ref, cos_ref, sin_ref,
                        qg_ref, kg_ref, e2_ref, w_ref, att_ref, kn_ref, kc_ref, vc_ref):
    lane = lax.broadcasted_iota(jnp.int32, (S_STEPS, LANES), 1)
    low_half = (lane % HD) < (HD // 2)
    e2 = e2_ref[...]
    qs, ks = [], []
    for t in range(GRP // LANES):
        cs = slice(t * LANES, (t + 1) * LANES)
        cos = cos_ref[:, cs]
        sin = sin_ref[:, cs]
        qs.append(_head_norm_rope(q_ref[:, cs], qg_ref[...], cos, sin, e2, low_half) * (HD ** -0.5))
        ks.append(_head_norm_rope(k_ref[:, cs], kg_ref[...], cos, sin, e2, low_half))
    q = jnp.concatenate(qs, axis=1)
    k = jnp.concatenate(ks, axis=1)
    kn_ref[...] = k

    kc_ref[0:N_CACHED, :] = ck_ref[...].astype(bf16)
    vc_ref[0:N_CACHED, :] = cv_ref[...].astype(bf16)
    kc_ref[N_CACHED:KPAD, :] = jnp.zeros((KPAD - N_CACHED, GRP), bf16)
    vc_ref[N_CACHED:KPAD, :] = jnp.zeros((KPAD - N_CACHED, GRP), bf16)
    kc_ref[N_CACHED:N_CACHED + 2 * S_STEPS, :] = jnp.concatenate(
        [k, jnp.zeros_like(k)], axis=0).astype(bf16)
    vc_ref[N_CACHED:N_CACHED + 2 * S_STEPS, :] = jnp.concatenate(
        [v_ref[...], jnp.zeros_like(k)], axis=0).astype(bf16)

    nq = (GRP // HD) * S_STEPS
    qt = jnp.concatenate([q] * (GRP // HD) + [jnp.zeros((LANES - nq, GRP), f32)], axis=0)
    row_h = lax.broadcasted_iota(jnp.int32, (LANES, GRP), 0) // S_STEPS
    lane_h = lax.broadcasted_iota(jnp.int32, (LANES, GRP), 1) // HD
    same_head = row_h == lane_h
    qt = jnp.where(same_head, qt, 0.0).astype(bf16)
    s = _dot_nt(kc_ref[...], qt)
    w = w_ref[...]
    keep = w > 0.0
    m = jnp.max(jnp.where(keep, s, -jnp.inf), axis=0, keepdims=True)
    e = jnp.where(keep, w * jnp.exp(s - m), 0.0)
    den = jnp.sum(e, axis=0, keepdims=True)
    pt = jnp.transpose(e / den).astype(bf16)
    res = _dot(pt, vc_ref[...])
    res = jnp.where(same_head, res, 0.0)
    out = res[0:S_STEPS]
    for h in range(1, GRP // HD):
        out = out + res[h * S_STEPS:(h + 1) * S_STEPS]
    att_ref[...] = out


def _attention_sample(proj, ck, cv, cos, sin, qg, kg, e2, wmask, layer):
    ng = D_ATT // GRP
    col = lambda off: pl.BlockSpec((S_STEPS, GRP), lambda b, g: (b, off + g))
    cache = pl.BlockSpec((None, None, N_CACHED, GRP), lambda b, g: (layer, b, 0, g))
    tab = pl.BlockSpec((S_STEPS, GRP), lambda b, g: (0, 0))
    gain = pl.BlockSpec((None, 1, LANES), lambda b, g: (layer, 0, 0))
    return pl.pallas_call(
        _attn_sample_kernel,
        grid=(DEC_BATCH, ng),
        in_specs=[col(0), col(ng), col(2 * ng), cache, cache, tab, tab, gain, gain,
                  pl.BlockSpec((LANES, LANES), lambda b, g: (0, 0)),
                  pl.BlockSpec((KPAD, LANES), lambda b, g: (0, 0))],
        out_specs=[pl.BlockSpec((S_STEPS, GRP), lambda b, g: (b, g))] * 2,
        out_shape=[jax.ShapeDtypeStruct((S_ROWS, D_ATT), f32)] * 2,
        scratch_shapes=[pltpu.VMEM((KPAD, GRP), bf16), pltpu.VMEM((KPAD, GRP), bf16)],
        compiler_params=_cparams(("arbitrary", "arbitrary")),
        name="attn_sample",
    )(proj, proj, proj, ck, cv, cos, sin, qg, kg, e2, wmask)


def _sample_key_weights():
    pos = np.full((KPAD,), -1, np.int64)
    pos[:N_CACHED + DEC_SEQ] = np.arange(N_CACHED + DEC_SEQ)
    w = np.zeros((KPAD, LANES), np.float32)
    for t in range(S_STEPS):
        mult = np.zeros((KPAD,), np.float32)
        if t < DEC_SEQ:
            dist = np.where(pos >= 0, MAX_WINDOW + t - pos, -1)
            for d in DILATIONS:
                mult += ((dist >= 0) & (dist % d == 0) & (dist <= WIN_KEYS * d)).astype(np.float32)
        else:
            mult[N_CACHED + t] = 1.0
        for h in range(GRP // HD):
            w[:, h * S_STEPS + t] = mult
    w[0, (GRP // HD) * S_STEPS:] = 1.0
    return w


def _ssd_kernel(z_ref, xbc_ref, dt_ref, ci_ref, h0_ref, cw_ref, cb_ref, dtb_ref, a_ref, dsk_ref,
                ng_ref, tri_ref, exp_ref, y_ref, ho_ref,
                cbuf_ref, xc_ref, st_ref, zb_ref, db_ref, yb_ref,
                *, padded, valid, has_h0):
    c = pl.program_id(1)
    nc = pl.num_programs(1)

    if padded:
        zb_ref[...] = jnp.zeros_like(zb_ref)
        db_ref[...] = jnp.zeros_like(db_ref)
        cbuf_ref[SUBLANES:, :] = jnp.zeros((CHUNK, D_XBC), f32)
        zb_ref[0:S_STEPS, :] = z_ref[...]
        db_ref[0:S_STEPS, :] = dt_ref[...]
        cbuf_ref[SUBLANES:SUBLANES + S_STEPS, :] = xbc_ref[...]
        zsrc, dsrc = zb_ref, db_ref
    else:
        cbuf_ref[SUBLANES:, :] = xbc_ref[...]
        zsrc, dsrc = z_ref, dt_ref

    @pl.when(c == 0)
    def _():
        cbuf_ref[0:SUBLANES, :] = ci_ref[...]
        if has_h0:
            st_ref[...] = jnp.transpose(h0_ref[...])
        else:
            st_ref[...] = jnp.zeros_like(st_ref)

    for t in range(D_XBC // 256):
        cs = slice(t * 256, (t + 1) * 256)
        acc = cb_ref[:, cs] + cbuf_ref[5:5 + CHUNK, cs] * cw_ref[0:1, cs]
        acc = acc + cbuf_ref[6:6 + CHUNK, cs] * cw_ref[1:2, cs]
        acc = acc + cbuf_ref[7:7 + CHUNK, cs] * cw_ref[2:3, cs]
        acc = acc + cbuf_ref[8:8 + CHUNK, cs] * cw_ref[3:4, cs]
        xc_ref[:, cs] = _silu(acc)
    cbuf_ref[0:SUBLANES, :] = cbuf_ref[CHUNK:CHUNK + SUBLANES, :]

    tri = tri_ref[...]
    expand = exp_ref[...]
    dt = _softplus(dsrc[...] + dtb_ref[...])
    a = dt * a_ref[...]
    a_cum = None
    for hi in _split(a, 3):
        t_ = _dot(tri, hi)
        a_cum = t_ if a_cum is None else a_cum + t_
    a_cum_t = jnp.transpose(a_cum)
    dt_e = _split_dot(dt, expand, 2)
    acum_e = _split_dot(a_cum, expand, 3)
    alast_e = acum_e[valid - 1:valid, :]
    row = lax.broadcasted_iota(jnp.int32, (CHUNK, D_SSD), 0)

    xs = xc_ref[:, 0:D_SSD]
    xdt = xs * dt_e
    xdt_b = xdt.astype(bf16)
    xend_b = jnp.where(row < valid, xdt * jnp.exp(alast_e - acum_e), 0.0).astype(bf16)

    ii = lax.broadcasted_iota(jnp.int32, (CHUNK, CHUNK), 0)
    jj = lax.broadcasted_iota(jnp.int32, (CHUNK, CHUNK), 1)
    causal = jj <= ii
    lane_lo = lax.broadcasted_iota(jnp.int32, (CHUNK, LANES), 1) < HD
    hpg = H_S // SSD_GROUPS
    gw = hpg * HD
    for g in range(SSD_GROUPS):
        gs = slice(g * gw, (g + 1) * gw)
        b0 = D_SSD + g * SSD_STATE
        c0 = D_SSD + (SSD_GROUPS + g) * SSD_STATE
        bm = xc_ref[:, b0:b0 + SSD_STATE]
        cm = xc_ref[:, c0:c0 + SSD_STATE].astype(bf16)
        cbm = _dot_nt(cm, bm.astype(bf16))
        bt = jnp.transpose(bm).astype(bf16)
        st_new = _dot(bt, xend_b[:, gs])
        y_off = _dot(cm, st_ref[:, gs].astype(bf16))
        for pr in range(hpg // 2):
            ys = []
            ps = slice(g * gw + pr * LANES, g * gw + (pr + 1) * LANES)
            for hh in range(2):
                h = g * hpg + pr * 2 + hh
                seg = a_cum[:, h:h + 1] - a_cum_t[h:h + 1, :]
                gm = (cbm * jnp.exp(jnp.where(causal, seg, -jnp.inf))).astype(bf16)
                ys.append(_dot(gm, xdt_b[:, ps]))
            yb_ref[:, ps] = jnp.where(lane_lo, ys[0], ys[1])
        yb_ref[:, gs] = yb_ref[:, gs] + y_off * jnp.exp(acum_e[:, gs])
        st_ref[:, gs] = jnp.exp(alast_e[:, gs]) * st_ref[:, gs] + st_new

    y = yb_ref[...] + dsk_ref[...] * xs
    y = y * _silu(zsrc[...])
    ms = jnp.mean(y * y, axis=-1, keepdims=True)
    y = y * lax.rsqrt(ms + EPS) * ng_ref[...]
    if padded:
        y_ref[...] = y[0:S_STEPS]
    else:
        y_ref[...] = y.astype(y_ref.dtype)

    @pl.when(c == nc - 1)
    def _():
        ho_ref[...] = jnp.transpose(st_ref[...])


def _ssd(proj, dt_raw, cinit, h0, cw, cb, dtb, a_neg, dskip, ng, tri, expand, layer, *, sample):
    if sample:
        nb, nc, rows = DEC_BATCH, 1, S_STEPS
        y_dtype = f32
        h0_spec = pl.BlockSpec((None, None, D_SSD, SSD_STATE), lambda b, c: (layer, b, 0, 0))
    else:
        nb, nc, rows = BATCH, SEQ // CHUNK, CHUNK
        y_dtype = bf16
        h0_spec = pl.BlockSpec((None, None, D_SSD, SSD_STATE), lambda b, c: (0, 0, 0, 0))
    rowblk = lambda width, colblk: pl.BlockSpec((rows, width), lambda b, c: (b * nc + c, colblk))
    vec = lambda width: pl.BlockSpec((None, 1, width), lambda b, c: (layer, 0, 0))
    const = lambda shape: pl.BlockSpec(shape, lambda b, c: (0, 0))
    kern = functools.partial(_ssd_kernel, padded=sample, valid=DEC_SEQ if sample else CHUNK,
                             has_h0=sample)
    return pl.pallas_call(
        kern,
        grid=(nb, nc),
        in_specs=[
            rowblk(D_SSD, COL_Z // D_SSD),
            rowblk(D_XBC, COL_XBC // D_XBC),
            rowblk(LANES, 0),
            pl.BlockSpec((None, SUBLANES, D_XBC), lambda b, c: (b, 0, 0)),
            h0_spec,
            pl.BlockSpec((None, SSD_CONV, D_XBC), lambda b, c: (layer, 0, 0)),
            vec(D_XBC), vec(LANES), vec(LANES), vec(D_SSD), vec(D_SSD),
            const((CHUNK, CHUNK)), const((LANES, D_SSD)),
        ],
        out_specs=[rowblk(D_SSD, 0), pl.BlockSpec((None, D_SSD, SSD_STATE), lambda b, c: (b, 0, 0))],
        out_shape=[jax.ShapeDtypeStruct((nb * nc * rows, D_SSD), y_dtype),
                   jax.ShapeDtypeStruct((nb, D_SSD, SSD_STATE), f32)],
        scratch_shapes=[
            pltpu.VMEM((CHUNK + SUBLANES, D_XBC), f32),
            pltpu.VMEM((CHUNK, D_XBC), f32),
            pltpu.VMEM((SSD_STATE, D_SSD), f32),
            pltpu.VMEM((CHUNK, D_SSD), f32),
            pltpu.VMEM((CHUNK, LANES), f32),
            pltpu.VMEM((CHUNK, D_SSD), f32),
        ],
        compiler_params=_cparams(("arbitrary", "arbitrary")),
        name="ssd_sample" if sample else "ssd_prompt",
    )(proj, proj, dt_raw, cinit, h0, cw, cb, dtb, a_neg, dskip, ng, tri, expand)


def _rope_tables(pos, width):
    half = HD // 2
    inv = ROPE_THETA ** (-jnp.arange(half, dtype=f32) / half)
    ang = pos.astype(f32)[:, None] * inv[None, :]
    cos = jnp.cos(ang)
    sin = jnp.sin(ang)
    cos_h = jnp.concatenate([cos, cos], axis=-1)
    sin_h = jnp.concatenate([-sin, sin], axis=-1)
    reps = width // HD
    return jnp.tile(cos_h, (1, reps)), jnp.tile(sin_h, (1, reps))


def kernel(x_prompt, x_sample, cache_win_k, cache_win_v, state_ssd_conv, state_ssd, state_ffn_conv,
           norm1_g, w_in, q_norm_g, k_norm_g, ssd_conv_w, ssd_conv_b, ssd_dt_bias, ssd_a_log,
           ssd_d, ssd_norm_g, w_out, norm2_g, w_up, ffn_conv_w, ffn_conv_b, w_down):
    w_in_b = w_in.astype(bf16)
    w_dt_b = jnp.pad(w_in[:, :, D_PROJ:], ((0, 0), (0, 0), (0, LANES - H_S))).astype(bf16)
    w_out_b = w_out.astype(bf16)
    w_up_b = w_up.astype(bf16)
    w_down_b = w_down.astype(bf16)
    g1 = norm1_g[:, None, :]
    g2 = norm2_g[:, None, :]
    qg = jnp.tile(q_norm_g, (1, LANES // HD))[:, None, :]
    kg = jnp.tile(k_norm_g, (1, LANES // HD))[:, None, :]
    cb_ssd = ssd_conv_b[:, None, :]
    lane_pad = ((0, 0), (0, LANES - H_S))
    dtb = jnp.pad(ssd_dt_bias, lane_pad)[:, None, :]
    a_neg = jnp.pad(-jnp.exp(ssd_a_log.astype(f32)), lane_pad)[:, None, :]
    dskip = jnp.repeat(ssd_d, HD, axis=1)[:, None, :]
    ng = ssd_norm_g[:, None, :]
    cb_ffn = ffn_conv_b[:, None, :]

    idx = np.arange(LANES)
    e2 = jnp.asarray((idx[:, None] // HD == idx[None, :] // HD).astype(np.float32), dtype=bf16)
    tri = jnp.asarray((idx[None, :] <= idx[:, None]).astype(np.float32), dtype=bf16)
    expand = jnp.asarray((idx[:, None] == np.arange(D_SSD)[None, :] // HD).astype(np.float32), dtype=bf16)
    wmask = jnp.asarray(_sample_key_weights())
    bias_band, bias_first = (jnp.asarray(a) for a in _attn_biases())
    cos_p, sin_p = _rope_tables(jnp.arange(SEQ), LANES)
    cos_s, sin_s = _rope_tables(PAST_LEN + jnp.arange(S_STEPS), GRP)

    xp = x_prompt.reshape(BATCH * SEQ, D_MODEL)
    xs = jnp.pad(x_sample, ((0, 0), (0, S_STEPS - DEC_SEQ), (0, 0))).reshape(S_ROWS, D_MODEL)

    ck = cache_win_k.reshape(DEPTH, DEC_BATCH, MAX_WINDOW, D_ATT)
    cv = cache_win_v.reshape(DEPTH, DEC_BATCH, MAX_WINDOW, D_ATT)
    ssd_ci_p = jnp.zeros((BATCH, SUBLANES, D_XBC), f32)
    ssd_ci_s = jnp.pad(state_ssd_conv, ((0, 0), (0, 0), (SUBLANES - (SSD_CONV - 1), 0), (0, 0)))
    h0_s = state_ssd.reshape(DEPTH, DEC_BATCH, D_SSD, SSD_STATE)
    h0_p = jnp.zeros((1, 1, D_SSD, SSD_STATE), f32)
    ffn_ci_p = jnp.zeros((BATCH, FFN_PAD, 2 * D_FF), f32)
    ffn_ci_s = jnp.pad(state_ffn_conv, ((0, 0), (0, 0), (SUBLANES - (FFN_CONV - 1), S_STEPS), (0, 0))
                       ).reshape(DEPTH, 1, DEC_BATCH * FFN_PAD, 2 * D_FF)

    tm_in, tm_p, tm_ffn = 1024, 512, 1024
    k_all = jnp.zeros((DEPTH, BATCH, SEQ, D_ATT), f32)
    v_all = jnp.zeros((DEPTH, BATCH, SEQ, D_ATT), f32)
    outs = {k: [] for k in ("ks", "vs", "cp", "cs", "hp", "hs", "fp", "fs")}
    for i in range(DEPTH):
        proj, dt_raw = _in_proj(xp, g1, w_in_b, w_dt_b, i, tm_in)
        proj3 = proj.reshape(BATCH, SEQ, D_PROJ)
        att, k_all, v_all = _attention_prompt(proj3, cos_p, sin_p, qg, kg, e2, bias_band, bias_first, i,
                                              k_all, v_all)
        ssd, h_last = _ssd(proj, dt_raw, ssd_ci_p, h0_p, ssd_conv_w, cb_ssd, dtb, a_neg, dskip, ng, tri,
                           expand, i, sample=False)
        x1 = _out_proj(att.reshape(BATCH * SEQ, D_ATT), ssd, w_out_b, xp, i, tm_p)
        xp, hs_g, hs_u = _ffn(x1, g2, w_up_b, ffn_conv_w, cb_ffn, w_down_b, ffn_ci_p, i, tm=tm_ffn,
                              tiles_per_seq=SEQ // tm_ffn, per_tile=False)
        outs["cp"].append(proj3[:, SEQ - (SSD_CONV - 1):, COL_XBC:COL_XBC + D_XBC])
        outs["hp"].append(h_last.reshape(BATCH, H_S, HD, SSD_STATE))
        last = slice(SEQ // tm_ffn - 1, None, SEQ // tm_ffn)
        hs = jnp.concatenate([hs_g[last], hs_u[last]], axis=-1)
        outs["fp"].append(hs[:, FFN_PAD - (FFN_CONV - 1):, :])

        proj_s, dt_s = _in_proj(xs, g1, w_in_b, w_dt_b, i, S_ROWS, tn=D_PROJ // 2)
        att_s, kn_s = _attention_sample(proj_s, ck, cv, cos_s, sin_s, qg, kg, e2, wmask, i)
        ssd_s, h_last_s = _ssd(proj_s, dt_s, ssd_ci_s[i], h0_s, ssd_conv_w, cb_ssd, dtb, a_neg, dskip, ng,
                               tri, expand, i, sample=True)
        x1_s = _out_proj(att_s, ssd_s, w_out_b, xs, i, S_ROWS)
        xs, hs_g, hs_u = _ffn(x1_s, g2, w_up_b, ffn_conv_w, cb_ffn, w_down_b, ffn_ci_s[i], i,
                              tm=S_ROWS, tiles_per_seq=1, per_tile=True)
        p3 = proj_s.reshape(DEC_BATCH, S_STEPS, D_PROJ)
        outs["ks"].append(kn_s.reshape(DEC_BATCH, S_STEPS, H_A, HD)[:, :DEC_SEQ])
        outs["vs"].append(p3[:, :DEC_SEQ, 2 * D_ATT:3 * D_ATT].reshape(DEC_BATCH, DEC_SEQ, H_A, HD))
        outs["cs"].append(p3[:, DEC_SEQ - (SSD_CONV - 1):DEC_SEQ, COL_XBC:COL_XBC + D_XBC])
        outs["hs"].append(h_last_s.reshape(DEC_BATCH, H_S, HD, SSD_STATE))
        hs = jnp.concatenate([hs_g[0], hs_u[0]], axis=-1).reshape(DEC_BATCH, FFN_PAD, 2 * D_FF)
        lo = SUBLANES + DEC_SEQ - (FFN_CONV - 1)
        outs["fs"].append(hs[:, lo:lo + FFN_CONV - 1, :])

    yp = xp.reshape(BATCH, SEQ, D_MODEL)
    ys = xs.reshape(DEC_BATCH, S_STEPS, D_MODEL)[:, :DEC_SEQ]
    st = lambda k: jnp.stack(outs[k])
    win_shape = (DEPTH, BATCH, SEQ, H_A, HD)
    return (yp, ys, k_all.reshape(win_shape), v_all.reshape(win_shape), st("ks"), st("vs"),
            st("cp"), st("cs"), st("hp"), st("hs"), st("fp"), st("fs"))
```
